```python
import math
import jax, jax.numpy as jnp
from jax import lax
import numpy as np

D_MODEL = 1024
BATCH = 8
SEQ = 2048
DEPTH = 4

N_SELF = DEPTH // 2
N_CROSS = DEPTH - N_SELF
N_DENSE = (DEPTH + 1) // 2
N_MOE = DEPTH // 2
CONV_WIDTH = 3
N_HEADS = 16
HEAD_DIM = D_MODEL // N_HEADS
BLOCK_Q = 128
D_FF = 2816
D_FF_EXPERT = 2816
N_EXPERTS = 8
TOP_K = 2
RMS_EPS = 1e-6

kernel_name = "yoco_shortconv_stickbreaking_moe"


def rms_norm(x, g):
    xf = x.astype(jnp.float32)
    y = xf * lax.rsqrt(jnp.mean(xf * xf, axis=-1, keepdims=True) + RMS_EPS)
    return (y * g.astype(jnp.float32)).astype(x.dtype)


def swiglu(h, w_gu, w_down):
    gate, up = jnp.split(h @ w_gu, 2, axis=-1)
    return (jax.nn.silu(gate) * up) @ w_down


def short_conv_mixer(h, w_in, conv_w, w_out):
    c_gate, val, b_gate = jnp.split(h @ w_in, 3, axis=-1)
    u = c_gate * val
    conv = lax.conv_general_dilated(
        u, conv_w[:, None, :].astype(u.dtype), window_strides=(1,),
        padding=[(CONV_WIDTH - 1, 0)], dimension_numbers=('NWC', 'WIO', 'NWC'),
        feature_group_count=D_MODEL)
    return (b_gate * conv) @ w_out


def stick_breaking_attention(q, k, v):
    seq = q.shape[1]
    scale = 1.0 / math.sqrt(HEAD_DIM)
    outs = []
    for blk in range(seq // BLOCK_Q):
        t0 = blk * BLOCK_Q
        t1 = t0 + BLOCK_Q
        qb = q[:, t0:t1].astype(jnp.float32)
        kb = k[:, :t1].astype(jnp.float32)
        vb = v[:, :t1]
        z = jnp.einsum('bqhd,bshd->bhqs', qb, kb) * scale
        t_idx = t0 + jnp.arange(BLOCK_Q)[:, None]
        s_idx = jnp.arange(t1)[None, :]
        causal = s_idx < t_idx
        log_1m_beta = jnp.where(causal, jax.nn.log_sigmoid(-z), 0.0)
        suffix = lax.cumsum(log_1m_beta, axis=3, reverse=True) - log_1m_beta
        attn = jnp.where(causal, jnp.exp(jax.nn.log_sigmoid(z) + suffix), 0.0)
        outs.append(jnp.einsum('bhqs,bshd->bqhd', attn.astype(vb.dtype), vb))
    return jnp.concatenate(outs, axis=1)


def moe_swiglu(h, w_router, w_gu, w_down):
    logits = (h @ w_router).astype(jnp.float32)
    top_val, top_idx = lax.top_k(logits, TOP_K)
    top_w = jax.nn.softmax(top_val, axis=-1)
    gates = jnp.sum(jax.nn.one_hot(top_idx, N_EXPERTS, dtype=jnp.float32) * top_w[..., None], axis=1)
    y = jnp.zeros_like(h)
    for e in range(N_EXPERTS):
        y = y + gates[:, e:e + 1].astype(h.dtype) * swiglu(h, w_gu[e], w_down[e])
    return y


def setup_inputs(seed: int = 0) -> dict:
    key = jax.random.key(seed)
    ks = jax.random.split(key, 16)
    f32 = jnp.float32
    d = D_MODEL

    def nrm(k, shape, fan_in):
        return jax.random.normal(k, shape, f32) * (fan_in ** -0.5)

    def gain(k, shape):
        return 1.0 + 0.02 * jax.random.normal(k, shape, f32)

    return {
        "x": jax.random.normal(ks[0], (BATCH, SEQ, d), f32),
        "g_mix": gain(ks[1], (DEPTH, d)),
        "g_ffn": gain(ks[2], (DEPTH, d)),
        "g_final": gain(ks[3], (d,)),
        "a_w_in": nrm(ks[4], (N_SELF, d, 3 * d), d),
        "a_conv_w": nrm(ks[5], (N_SELF, CONV_WIDTH, d), CONV_WIDTH),
        "a_w_out": nrm(ks[6], (N_SELF, d, d), d),
        "g_kv": gain(ks[7], (d,)),
        "w_kv": nrm(ks[8], (d, 2 * d), d),
        "b_w_q": nrm(ks[9], (N_CROSS, d, d), d),
        "b_w_o": nrm(ks[10], (N_CROSS, d, d), d),
        "ffn_w_gu": nrm(ks[11], (N_DENSE, d, 2 * D_FF), d),
        "ffn_w_down": nrm(ks[12], (N_DENSE, D_FF, d), D_FF),
        "moe_w_router": nrm(ks[13], (N_MOE, d, N_EXPERTS), d),
        "moe_w_gu": nrm(ks[14], (N_MOE, N_EXPERTS, d, 2 * D_FF_EXPERT), d),
        "moe_w_down": nrm(ks[15], (N_MOE, N_EXPERTS, D_FF_EXPERT, d), D_FF_EXPERT),
    }


def reference(x, g_mix, g_ffn, g_final, a_w_in, a_conv_w, a_w_out, g_kv, w_kv,
              b_w_q, b_w_o, ffn_w_gu, ffn_w_down, moe_w_router, moe_w_gu, moe_w_down):
    bsz, seq, d = x.shape
    k_shared = None
    v_shared = None
    for i in range(DEPTH):
        if i < N_SELF:
            h = rms_norm(x, g_mix[i])
            x = x + short_conv_mixer(h, a_w_in[i], a_conv_w[i], a_w_out[i])
        else:
            if i == N_SELF:
                kv = rms_norm(x, g_kv) @ w_kv
                k_shared, v_shared = jnp.split(kv, 2, axis=-1)
                k_shared = k_shared.reshape(bsz, seq, N_HEADS, HEAD_DIM)
                v_shared = v_shared.reshape(bsz, seq, N_HEADS, HEAD_DIM)
            j = i - N_SELF
            h = rms_norm(x, g_mix[i])
            q = (h @ b_w_q[j]).reshape(bsz, seq, N_HEADS, HEAD_DIM)
            o = stick_breaking_attention(q, k_shared, v_shared).reshape(bsz, seq, d)
            x = x + o @ b_w_o[j]
        h = rms_norm(x, g_ffn[i])
        if i % 2 == 0:
            x = x + swiglu(h, ffn_w_gu[i // 2], ffn_w_down[i // 2])
        else:
            m = i // 2
            y = moe_swiglu(h.reshape(bsz * seq, d), moe_w_router[m], moe_w_gu[m], moe_w_down[m])
            x = x + y.reshape(bsz, seq, d)
    return rms_norm(x, g_final)
```

```python
import functools
import math

import jax
import jax.numpy as jnp
from jax import lax
from jax.experimental import pallas as pl
from jax.experimental.pallas import tpu as pltpu

F32 = jnp.float32
BF16 = jnp.bfloat16

RMS_EPS = 1e-6
N_HEADS = 16
N_EXPERTS = 8
TOP_K = 2

V7X_LANES = 128
V7X_SUBLANES = 8
V7X_VMEM_BYTES = 64 * 1024 * 1024
VMEM_LIMIT = V7X_VMEM_BYTES - 8 * 1024 * 1024

MIXER_ROWS = 512
FFN_ROWS = 512
PROJ_ROWS = 1024
ROUTER_ROWS = 512
EXPERT_ROWS = 512
ATTN_BLOCK = 256
HEADS_PER_STEP = 2
CONV_CARRY_ROWS = V7X_SUBLANES


def _params(*semantics):
    return pltpu.CompilerParams(dimension_semantics=semantics,
                                vmem_limit_bytes=VMEM_LIMIT)


def _dot(a, b):
    return jnp.dot(a, b, preferred_element_type=F32)


def _rms_unit(x):
    return x * lax.rsqrt(jnp.mean(x * x, axis=-1, keepdims=True) + RMS_EPS)


def _load_token_major(ref, rows):
    parts = [ref[pl.ds(j, rows, stride=V7X_SUBLANES), :] for j in range(V7X_SUBLANES)]
    return jnp.concatenate(parts, axis=1)


def _store_token_major(ref, val, rows):
    for j in range(V7X_SUBLANES):
        ref[pl.ds(j, rows, stride=V7X_SUBLANES), :] = val[:, j * V7X_LANES:(j + 1) * V7X_LANES]


def _mixer_kernel(x_ref, g_ref, win_ref, cw_ref, wout_ref, out_ref, carry_ref, *, rows, d):
    s = pl.program_id(1)

    @pl.when(s == 0)
    def _():
        carry_ref[...] = jnp.zeros_like(carry_ref)

    x = x_ref[...]
    h = (_rms_unit(x) * g_ref[...]).astype(BF16)
    proj = _dot(h, win_ref[...])
    u = proj[:, :d] * proj[:, d:2 * d]
    b_gate = proj[:, 2 * d:]
    prev = carry_ref[...]
    row = lax.broadcasted_iota(jnp.int32, (rows, d), 0)
    u1 = jnp.where(row == 0, prev[CONV_CARRY_ROWS - 1:CONV_CARRY_ROWS, :], pltpu.roll(u, 1, 0))
    u2 = pltpu.roll(u, 2, 0)
    u2 = jnp.where(row == 0, prev[CONV_CARRY_ROWS - 2:CONV_CARRY_ROWS - 1, :], u2)
    u2 = jnp.where(row == 1, prev[CONV_CARRY_ROWS - 1:CONV_CARRY_ROWS, :], u2)
    carry_ref[...] = u[rows - CONV_CARRY_ROWS:, :]
    cw = cw_ref[...]
    conv = cw[0:1, :] * u2 + cw[1:2, :] * u1 + cw[2:3, :] * u
    mixed = (b_gate * conv).astype(BF16)
    out_ref[...] = x + _dot(mixed, wout_ref[...])


def _mixer(x, g, w_in, conv_w, w_out):
    b, s, d = x.shape
    rows = min(MIXER_ROWS, s)
    return pl.pallas_call(
        functools.partial(_mixer_kernel, rows=rows, d=d),
        grid=(b, s // rows),
        in_specs=[
            pl.BlockSpec((None, rows, d), lambda i, j: (i, j, 0)),
            pl.BlockSpec((1, d), lambda i, j: (0, 0)),
            pl.BlockSpec((d, 3 * d), lambda i, j: (0, 0)),
            pl.BlockSpec((3, d), lambda i, j: (0, 0)),
            pl.BlockSpec((d, d), lambda i, j: (0, 0)),
        ],
        out_specs=pl.BlockSpec((None, rows, d), lambda i, j: (i, j, 0)),
        out_shape=jax.ShapeDtypeStruct(x.shape, F32),
        scratch_shapes=[pltpu.VMEM((CONV_CARRY_ROWS, d), F32)],
        compiler_params=_params("arbitrary", "arbitrary"),
        name="conv_mixer",
    )(x, g.reshape(1, d), w_in, conv_w, w_out)


def _ffn_kernel(*refs, has_o):
    if has_o:
        x_ref, o_ref, wo_ref, g_ref, wg_ref, wu_ref, wd_ref, out_ref, h_ref = refs
    else:
        x_ref, g_ref, wg_ref, wu_ref, wd_ref, out_ref, h_ref = refs

    @pl.when(pl.program_id(1) == 0)
    def _():
        x = x_ref[...]
        if has_o:
            x = x + _dot(o_ref[...], wo_ref[...])
        out_ref[...] = x
        h_ref[...] = (_rms_unit(x) * g_ref[...]).astype(BF16)

    h = h_ref[...]
    gate = _dot(h, wg_ref[...])
    up = _dot(h, wu_ref[...])
    act = (gate * jax.nn.sigmoid(gate) * up).astype(BF16)
    out_ref[...] += _dot(act, wd_ref[...])


def _ff_tile(d_ff):
    half = d_ff // 2
    return half if half % V7X_LANES == 0 else d_ff


def _ffn(x, g, w_gu, w_down, o=None, w_o=None):
    n, d = x.shape
    d_ff = w_down.shape[0]
    rows = min(FFN_ROWS, n)
    tf = _ff_tile(d_ff)
    n_f = d_ff // tf
    has_o = o is not None
    row_spec = pl.BlockSpec((rows, d), lambda i, f: (i, 0))
    in_specs = [row_spec]
    args = [x]
    if has_o:
        in_specs += [row_spec, pl.BlockSpec((d, d), lambda i, f: (0, 0))]
        args += [o, w_o]
    in_specs += [
        pl.BlockSpec((1, d), lambda i, f: (0, 0)),
        pl.BlockSpec((d, tf), lambda i, f: (0, f)),
        pl.BlockSpec((d, tf), lambda i, f: (0, f + n_f)),
        pl.BlockSpec((tf, d), lambda i, f: (f, 0)),
    ]
    args += [g.reshape(1, d), w_gu, w_gu, w_down]
    return pl.pallas_call(
        functools.partial(_ffn_kernel, has_o=has_o),
        grid=(n // rows, n_f),
        in_specs=in_specs,
        out_specs=row_spec,
        out_shape=jax.ShapeDtypeStruct((n, d), F32),
        scratch_shapes=[pltpu.VMEM((rows, d), BF16)],
        compiler_params=_params("arbitrary", "arbitrary"),
        name="dense_swiglu",
    )(*args)


def _proj_kernel(*refs, has_y, has_kv, rows, d, q_scale):
    refs = list(refs)
    x_ref = refs.pop(0)
    if has_y:
        y0_ref = refs.pop(0)
        y1_ref = refs.pop(0)
    gq_ref = refs.pop(0)
    wq_ref = refs.pop(0)
    if has_kv:
        gkv_ref = refs.pop(0)
        wkv_ref = refs.pop(0)
    if has_y:
        xs_ref = refs.pop(0)
    q_ref = refs.pop(0)
    if has_kv:
        k_ref = refs.pop(0)
        v_ref = refs.pop(0)

    x = x_ref[...]
    if has_y:
        x = x + _load_token_major(y0_ref, rows) + _load_token_major(y1_ref, rows)
        xs_ref[...] = x
    xn = _rms_unit(x)
    q = _dot((xn * gq_ref[...]).astype(BF16), wq_ref[...])
    q_ref[...] = (q * q_scale).astype(BF16)
    if has_kv:
        kv = _dot((xn * gkv_ref[...]).astype(BF16), wkv_ref[...])
        k_ref[...] = kv[:, :d].astype(BF16)
        v_ref[...] = kv[:, d:].astype(BF16)


def _proj(x, g_q, w_q, q_scale, y=None, g_kv=None, w_kv=None):
    n, d = x.shape
    rows = min(PROJ_ROWS, n)
    has_y = y is not None
    has_kv = w_kv is not None
    row_spec = pl.BlockSpec((rows, d), lambda i: (i, 0))
    vec_spec = pl.BlockSpec((1, d), lambda i: (0, 0))
    in_specs = [row_spec]
    args = [x]
    if has_y:
        blocks_per_slot = n // rows
        tm_spec0 = pl.BlockSpec((rows * V7X_SUBLANES, V7X_LANES), lambda i: (i, 0))
        tm_spec1 = pl.BlockSpec((rows * V7X_SUBLANES, V7X_LANES),
                                lambda i: (i + blocks_per_slot, 0))
        in_specs += [tm_spec0, tm_spec1]
        args += [y, y]
    in_specs += [vec_spec, pl.BlockSpec((d, d), lambda i: (0, 0))]
    args += [g_q.reshape(1, d), w_q]
    if has_kv:
        in_specs += [vec_spec, pl.BlockSpec((d, 2 * d), lambda i: (0, 0))]
        args += [g_kv.reshape(1, d), w_kv]
    out_specs = []
    out_shape = []
    if has_y:
        out_specs.append(row_spec)
        out_shape.append(jax.ShapeDtypeStruct((n, d), F32))
    n_bf16 = 3 if has_kv else 1
    out_specs += [row_spec] * n_bf16
    out_shape += [jax.ShapeDtypeStruct((n, d), BF16)] * n_bf16
    return pl.pallas_call(
        functools.partial(_proj_kernel, has_y=has_y, has_kv=has_kv, rows=rows, d=d,
                          q_scale=q_scale),
        grid=(n // rows,),
        in_specs=in_specs,
        out_specs=out_specs,
        out_shape=out_shape,
        compiler_params=_params("arbitrary"),
        name="norm_qkv_proj",
    )(*args)


def _attn_kernel(q_ref, k_ref, v_ref, o_ref, acc_ref, run_ref, *, blk, dh):
    qi = pl.program_id(2)
    q = q_ref[...]
    lane = lax.broadcasted_iota(jnp.int32, (1, HEADS_PER_STEP * dh), 1)
    row = lax.broadcasted_iota(jnp.int32, (blk, blk), 0)
    col = lax.broadcasted_iota(jnp.int32, (blk, blk), 1)
    later = (row > col).astype(BF16)
    causal = col < row
    head_mask = [(lane >= hh * dh) & (lane < (hh + 1) * dh) for hh in range(HEADS_PER_STEP)]
    q_heads = [jnp.where(m, q, jnp.zeros_like(q)) for m in head_mask]

    acc_ref[...] = jnp.zeros_like(acc_ref)
    run_ref[...] = jnp.zeros_like(run_ref)

    def visit(kb, masked):
        start = pl.multiple_of(kb * blk, blk)
        k_blk = k_ref[pl.ds(start, blk), :]
        v_blk = v_ref[pl.ds(start, blk), :]
        for hh in range(HEADS_PER_STEP):
            z = lax.dot_general(q_heads[hh], k_blk, (((1,), (1,)), ((), ())),
                                preferred_element_type=F32)
            soft = jnp.log(1.0 + jnp.exp(-jnp.abs(z)))
            log_beta = jnp.minimum(z, 0.0) - soft
            log_1m_beta = jnp.minimum(-z, 0.0) - soft
            if masked:
                log_1m_beta = jnp.where(causal, log_1m_beta, 0.0)
            hi = log_1m_beta.astype(BF16)
            lo = (log_1m_beta - hi.astype(F32)).astype(BF16)
            suffix = _dot(hi, later) + _dot(lo, later)
            run = run_ref[hh]
            attn = jnp.exp(log_beta + suffix + run)
            if masked:
                attn = jnp.where(causal, attn, 0.0)
            v_head = jnp.where(head_mask[hh], v_blk, jnp.zeros_like(v_blk))
            acc_ref[...] += _dot(attn.astype(BF16), v_head)
            run_ref[hh] = run + jnp.sum(log_1m_beta, axis=-1, keepdims=True)

    visit(qi, True)

    def body(it, carry):
        visit(qi - 1 - it, False)
        return carry

    lax.fori_loop(0, qi, body, 0)
    o_ref[...] = acc_ref[...].astype(BF16)


def _attention(q, k, v, batch, seq):
    n, d = q.shape
    dh = d // N_HEADS
    blk = min(ATTN_BLOCK, seq)
    width = HEADS_PER_STEP * dh
    q3, k3, v3 = (a.reshape(batch, seq, d) for a in (q, k, v))
    out = pl.pallas_call(
        functools.partial(_attn_kernel, blk=blk, dh=dh),
        grid=(batch, d // width, seq // blk),
        in_specs=[
            pl.BlockSpec((None, blk, width), lambda b, h, i: (b, i, h)),
            pl.BlockSpec((None, seq, width), lambda b, h, i: (b, 0, h)),
            pl.BlockSpec((None, seq, width), lambda b, h, i: (b, 0, h)),
        ],
        out_specs=pl.BlockSpec((None, blk, width), lambda b, h, i: (b, i, h)),
        out_shape=jax.ShapeDtypeStruct((batch, seq, d), BF16),
        scratch_shapes=[pltpu.VMEM((blk, width), F32),
                        pltpu.VMEM((HEADS_PER_STEP, blk, 1), F32)],
        compiler_params=_params("arbitrary", "arbitrary", "arbitrary"),
        name="stick_breaking_attention",
    )(q3, k3, v3)
    return out.reshape(n, d)


def _router_kernel(*refs, has_o, rows):
    refs = list(refs)
    x_ref = refs.pop(0)
    if has_o:
        o_ref = refs.pop(0)
        wo_ref = refs.pop(0)
    g_ref = refs.pop(0)
    wr_hi_ref = refs.pop(0)
    wr_lo_ref = refs.pop(0)
    if has_o:
        xs_ref = refs.pop(0)
    h_ref, idx_ref, w_ref = refs

    x = x_ref[...]
    if has_o:
        x = x + _dot(o_ref[...], wo_ref[...])
        xs_ref[...] = x
    h = _rms_unit(x) * g_ref[...]
    _store_token_major(h_ref, h, rows)

    h_hi = h.astype(BF16)
    h_lo = (h - h_hi.astype(F32)).astype(BF16)
    logits = _dot(h_hi, wr_hi_ref[...]) + (_dot(h_lo, wr_hi_ref[...]) + _dot(h_hi, wr_lo_ref[...]))

    lane = lax.broadcasted_iota(jnp.int32, logits.shape, 1)
    lane_f = lane.astype(F32)
    neg_inf = jnp.float32(-jnp.inf)
    no_lane = jnp.float32(V7X_LANES)
    lg = jnp.where(lane < N_EXPERTS, logits, neg_inf)
    m1 = jnp.max(lg, axis=-1, keepdims=True)
    i1 = jnp.min(jnp.where(lg == m1, lane_f, no_lane), axis=-1, keepdims=True)
    lg2 = jnp.where(lane_f == i1, neg_inf, lg)
    m2 = jnp.max(lg2, axis=-1, keepdims=True)
    i2 = jnp.min(jnp.where(lg2 == m2, lane_f, no_lane), axis=-1, keepdims=True)
    e2 = jnp.exp(m2 - m1)
    w1 = 1.0 / (1.0 + e2)
    w2 = e2 * w1
    idx_ref[...] = jnp.where(lane == 0, i1, jnp.where(lane == 1, i2, 0.0)).astype(jnp.int32)
    w_ref[...] = jnp.where(lane == 0, w1, jnp.where(lane == 1, w2, 0.0))


def _router(x, g, w_router, o=None, w_o=None):
    n, d = x.shape
    rows = min(ROUTER_ROWS, n)
    has_o = o is not None
    wr = jnp.zeros((d, V7X_LANES), F32).at[:, :N_EXPERTS].set(w_router)
    wr_hi = wr.astype(BF16)
    wr_lo = (wr - wr_hi.astype(F32)).astype(BF16)
    row_spec = pl.BlockSpec((rows, d), lambda i: (i, 0))
    lane_spec = pl.BlockSpec((rows, V7X_LANES), lambda i: (i, 0))
    in_specs = [row_spec]
    args = [x]
    if has_o:
        in_specs += [row_spec, pl.BlockSpec((d, d), lambda i: (0, 0))]
        args += [o, w_o]
    in_specs += [pl.BlockSpec((1, d), lambda i: (0, 0)),
                 pl.BlockSpec((d, V7X_LANES), lambda i: (0, 0)),
                 pl.BlockSpec((d, V7X_LANES), lambda i: (0, 0))]
    args += [g.reshape(1, d), wr_hi, wr_lo]
    out_specs = []
    out_shape = []
    if has_o:
        out_specs.append(row_spec)
        out_shape.append(jax.ShapeDtypeStruct((n, d), F32))
    out_specs += [pl.BlockSpec((rows * V7X_SUBLANES, V7X_LANES), lambda i: (i, 0)),
                  lane_spec, lane_spec]
    out_shape += [jax.ShapeDtypeStruct((n * V7X_SUBLANES, V7X_LANES), F32),
                  jax.ShapeDtypeStruct((n, V7X_LANES), jnp.int32),
                  jax.ShapeDtypeStruct((n, V7X_LANES), F32)]
    outs = pl.pallas_call(
        functools.partial(_router_kernel, has_o=has_o, rows=rows),
        grid=(n // rows,),
        in_specs=in_specs,
        out_specs=out_specs,
        out_shape=out_shape,
        compiler_params=_params("arbitrary"),
        name="router_top2",
    )(*args)
    if has_o:
        xs, h_tm, idx, w = outs
    else:
        h_tm, idx, w = outs
        xs = x
    return xs, h_tm, idx[:, :TOP_K], w[:, :TOP_K]


def _plan_routing(idx, w, rows):
    n = idx.shape[0]
    n_pairs = n * TOP_K
    total = n_pairs + N_EXPERTS * rows
    e_flat = idx.reshape(n_pairs)
    order = jnp.argsort(e_flat, stable=True).astype(jnp.int32)
    experts = jnp.arange(N_EXPERTS, dtype=jnp.int32)
    counts = jnp.sum((e_flat[:, None] == experts[None, :]).astype(jnp.int32), axis=0)
    starts = jnp.cumsum(counts) - counts
    pad_counts = ((counts + rows - 1) // rows) * rows
    pad_ends = jnp.cumsum(pad_counts)
    pad_starts = pad_ends - pad_counts
    r = jnp.arange(total, dtype=jnp.int32)
    e_of_r = jnp.sum((r[:, None] >= pad_ends[None, :]).astype(jnp.int32), axis=1)
    e_clamped = jnp.minimum(e_of_r, N_EXPERTS - 1)
    within = r - pad_starts[e_clamped]
    valid = (e_of_r < N_EXPERTS) & (within < counts[e_clamped])
    pair = order[jnp.clip(starts[e_clamped] + within, 0, n_pairs - 1)]
    token = pair // TOP_K
    slot = pair % TOP_K
    src = jnp.where(valid, token, 0)
    dst = jnp.where(valid, slot * n + token, 0)
    gate = jnp.where(valid, w.reshape(n_pairs)[pair], 0.0)
    n_tiles = total // rows
    tile_expert = e_clamped[::rows]
    tile_count = jnp.sum(valid.reshape(n_tiles, rows).astype(jnp.int32), axis=1)
    return (src.reshape(n_tiles, 1, rows), dst.reshape(n_tiles, 1, rows),
            gate.reshape(total, 1), tile_expert, tile_count)


def _expert_kernel(te_ref, tc_ref, src_ref, dst_ref, gate_ref, h_hbm, wg_ref, wu_ref, wd_ref,
                   y_hbm, gbuf_ref, ybuf_ref, hbf_ref, acc_ref, gsem, ssem, *, rows, n_f):
    i = pl.program_id(0)
    f = pl.program_id(1)
    count = tc_ref[i]
    valid = count > 0

    def row_window(r):
        return pl.ds(pl.multiple_of(r * V7X_SUBLANES, V7X_SUBLANES), V7X_SUBLANES)

    def gather_copy(r):
        return pltpu.make_async_copy(h_hbm.at[row_window(src_ref[0, r]), :],
                                     gbuf_ref.at[row_window(r), :], gsem.at[r])

    def scatter_copy(r):
        return pltpu.make_async_copy(ybuf_ref.at[row_window(r), :],
                                     y_hbm.at[row_window(dst_ref[0, r]), :], ssem.at[r])

    def for_rows(n_rows, fn):
        def body(r, carry):
            fn(r)
            return carry
        lax.fori_loop(0, n_rows, body, 0)

    @pl.when(valid & (f == 0))
    def _():
        for_rows(rows, lambda r: gather_copy(r).start())
        for_rows(rows, lambda r: gather_copy(r).wait())
        hbf_ref[...] = _load_token_major(gbuf_ref, rows).astype(BF16)
        acc_ref[...] = jnp.zeros_like(acc_ref)

    @pl.when(valid)
    def _():
        h = hbf_ref[...]
        gate = _dot(h, wg_ref[...])
        up = _dot(h, wu_ref[...])
        act = (gate * jax.nn.sigmoid(gate) * up).astype(BF16)
        acc_ref[...] += _dot(act, wd_ref[...])

    @pl.when(valid & (f == n_f - 1))
    def _():
        _store_token_major(ybuf_ref, acc_ref[...] * gate_ref[...], rows)
        for_rows(count, lambda r: scatter_copy(r).start())
        for_rows(count, lambda r: scatter_copy(r).wait())


def _experts(h_tm, idx, w, w_gu, w_down, n):
    d = w_down.shape[2]
    d_ff = w_down.shape[1]
    rows = min(EXPERT_ROWS, n)
    tf = _ff_tile(d_ff)
    n_f = d_ff // tf
    src, dst, gate, tile_expert, tile_count = _plan_routing(idx, w, rows)
    n_tiles = src.shape[0]

    def f_eff(i, f, tc):
        return jnp.where(tc[i] > 0, f, n_f - 1)

    grid_spec = pltpu.PrefetchScalarGridSpec(
        num_scalar_prefetch=2,
        grid=(n_tiles, n_f),
        in_specs=[
            pl.BlockSpec((None, 1, rows), lambda i, f, te, tc: (i, 0, 0), memory_space=pltpu.SMEM),
            pl.BlockSpec((None, 1, rows), lambda i, f, te, tc: (i, 0, 0), memory_space=pltpu.SMEM),
            pl.BlockSpec((rows, 1), lambda i, f, te, tc: (i, 0)),
            pl.BlockSpec(memory_space=pl.ANY),
            pl.BlockSpec((None, d, tf), lambda i, f, te, tc: (te[i], 0, f_eff(i, f, tc))),
            pl.BlockSpec((None, d, tf), lambda i, f, te, tc: (te[i], 0, f_eff(i, f, tc) + n_f)),
            pl.BlockSpec((None, tf, d), lambda i, f, te, tc: (te[i], f_eff(i, f, tc), 0)),
        ],
        out_specs=pl.BlockSpec(memory_space=pl.ANY),
        scratch_shapes=[
            pltpu.VMEM((rows * V7X_SUBLANES, V7X_LANES), F32),
            pltpu.VMEM((rows * V7X_SUBLANES, V7X_LANES), F32),
            pltpu.VMEM((rows, d), BF16),
            pltpu.VMEM((rows, d), F32),
            pltpu.SemaphoreType.DMA((rows,)),
            pltpu.SemaphoreType.DMA((rows,)),
        ],
    )
    return pl.pallas_call(
        functools.partial(_expert_kernel, rows=rows, n_f=n_f),
        grid_spec=grid_spec,
        out_shape=jax.ShapeDtypeStruct((n * TOP_K * V7X_SUBLANES, V7X_LANES), F32),
        compiler_params=_params("arbitrary", "arbitrary"),
        name="routed_swiglu",
    )(tile_expert, tile_count, src, dst, gate, h_tm, w_gu, w_gu, w_down)


def _final_kernel(x_ref, y0_ref, y1_ref, g_ref, out_ref, *, rows):
    x = x_ref[...] + _load_token_major(y0_ref, rows) + _load_token_major(y1_ref, rows)
    out_ref[...] = _rms_unit(x) * g_ref[...]


def _final_norm(x, y, g):
    n, d = x.shape
    rows = min(PROJ_ROWS, n)
    blocks_per_slot = n // rows
    row_spec = pl.BlockSpec((rows, d), lambda i: (i, 0))
    return pl.pallas_call(
        functools.partial(_final_kernel, rows=rows),
        grid=(n // rows,),
        in_specs=[
            row_spec,
            pl.BlockSpec((rows * V7X_SUBLANES, V7X_LANES), lambda i: (i, 0)),
            pl.BlockSpec((rows * V7X_SUBLANES, V7X_LANES), lambda i: (i + blocks_per_slot, 0)),
            pl.BlockSpec((1, d), lambda i: (0, 0)),
        ],
        out_specs=row_spec,
        out_shape=jax.ShapeDtypeStruct((n, d), F32),
        compiler_params=_params("arbitrary"),
        name="final_norm",
    )(x, y, y, g.reshape(1, d))


def kernel(x, g_mix, g_ffn, g_final, a_w_in, a_conv_w, a_w_out, g_kv, w_kv, b_w_q, b_w_o,
           ffn_w_gu, ffn_w_down, moe_w_router, moe_w_gu, moe_w_down):
    batch, seq, d = x.shape
    n = batch * seq
    depth = g_mix.shape[0]
    n_self = a_w_in.shape[0]
    q_scale = 1.0 / math.sqrt(d // N_HEADS)

    bf = lambda a: a.astype(BF16)
    a_w_in, a_w_out, w_kv, b_w_q, b_w_o = map(bf, (a_w_in, a_w_out, w_kv, b_w_q, b_w_o))
    ffn_w_gu, ffn_w_down, moe_w_gu, moe_w_down = map(bf, (ffn_w_gu, ffn_w_down, moe_w_gu, moe_w_down))

    xs = x.reshape(n, d)
    y = None
    k = v = None
    for i in range(depth):
        o = w_o = None
        if i < n_self:
            assert y is None
            xs = _mixer(xs.reshape(batch, seq, d), g_mix[i], a_w_in[i], a_conv_w[i],
                        a_w_out[i]).reshape(n, d)
        else:
            j = i - n_self
            first = i == n_self
            outs = _proj(xs, g_mix[i], b_w_q[j], q_scale, y=y,
                         g_kv=g_kv if first else None, w_kv=w_kv if first else None)
            outs = list(outs)
            if y is not None:
                xs = outs.pop(0)
                y = None
            q = outs.pop(0)
            if first:
                k, v = outs
            o = _attention(q, k, v, batch, seq)
            w_o = b_w_o[j]
        assert y is None
        if i % 2 == 0:
            xs = _ffn(xs, g_ffn[i], ffn_w_gu[i // 2], ffn_w_down[i // 2], o=o, w_o=w_o)
        else:
            m = i // 2
            xs, h_tm, idx, w = _router(xs, g_ffn[i], moe_w_router[m], o=o, w_o=w_o)
            y = _experts(h_tm, idx, w, moe_w_gu[m], moe_w_down[m], n)
    assert y is not None
    return _final_norm(xs, y, g_final).reshape(batch, seq, d)
```

```python
import functools
import math

import jax
import jax.numpy as jnp
from jax import lax
from jax.experimental import pallas as pl
from jax.experimental.pallas import tpu as pltpu

F32 = jnp.float32
BF16 = jnp.bfloat16

RMS_EPS = 1e-6
LOG2_E = math.log2(math.e)
N_HEADS = 16
N_EXPERTS = 8
TOP_K = 2

V7X_LANES = 128
V7X_SUBLANES = 8
V7X_VMEM_BYTES = 64 * 1024 * 1024
VMEM_LIMIT = V7X_VMEM_BYTES - 8 * 1024 * 1024

MIXER_ROWS = 512
FFN_ROWS = 512
PROJ_ROWS = 1024
ROUTER_ROWS = 512
EXPERT_ROWS = 512
ATTN_BLOCK = 256
ATTN_GROUP = 4
HEADS_PER_STEP = 2
CONV_CARRY_ROWS = V7X_SUBLANES
DMA_UNROLL = 64


def _params(*semantics):
    return pltpu.CompilerParams(dimension_semantics=semantics,
                                vmem_limit_bytes=VMEM_LIMIT)


def _dot(a, b):
    return jnp.dot(a, b, preferred_element_type=F32)


def _rms_unit(x):
    return x * lax.rsqrt(jnp.mean(x * x, axis=-1, keepdims=True) + RMS_EPS)


def _load_token_major(ref, rows):
    parts = [ref[pl.ds(j, rows, stride=V7X_SUBLANES), :] for j in range(V7X_SUBLANES)]
    return jnp.concatenate(parts, axis=1)


def _store_token_major(ref, val, rows):
    for j in range(V7X_SUBLANES):
        ref[pl.ds(j, rows, stride=V7X_SUBLANES), :] = val[:, j * V7X_LANES:(j + 1) * V7X_LANES]


def _mixer_kernel(x_ref, g_ref, win_ref, cw_ref, wout_ref, out_ref, carry_ref, *, rows, d):
    s = pl.program_id(1)

    @pl.when(s == 0)
    def _():
        carry_ref[...] = jnp.zeros_like(carry_ref)

    x = x_ref[...]
    h = (_rms_unit(x) * g_ref[...]).astype(BF16)
    proj = _dot(h, win_ref[...])
    u = proj[:, :d] * proj[:, d:2 * d]
    b_gate = proj[:, 2 * d:]
    prev = carry_ref[...]
    row = lax.broadcasted_iota(jnp.int32, (rows, d), 0)
    u1 = jnp.where(row == 0, prev[CONV_CARRY_ROWS - 1:CONV_CARRY_ROWS, :], pltpu.roll(u, 1, 0))
    u2 = pltpu.roll(u, 2, 0)
    u2 = jnp.where(row == 0, prev[CONV_CARRY_ROWS - 2:CONV_CARRY_ROWS - 1, :], u2)
    u2 = jnp.where(row == 1, prev[CONV_CARRY_ROWS - 1:CONV_CARRY_ROWS, :], u2)
    carry_ref[...] = u[rows - CONV_CARRY_ROWS:, :]
    cw = cw_ref[...]
    conv = cw[0:1, :] * u2 + cw[1:2, :] * u1 + cw[2:3, :] * u
    mixed = (b_gate * conv).astype(BF16)
    out_ref[...] = x + _dot(mixed, wout_ref[...])


def _mixer(x, g, w_in, conv_w, w_out, batch, seq):
    n, d = x.shape
    rows = min(MIXER_ROWS, seq)
    tiles = seq // rows
    return pl.pallas_call(
        functools.partial(_mixer_kernel, rows=rows, d=d),
        grid=(batch, tiles),
        in_specs=[
            pl.BlockSpec((rows, d), lambda i, j: (i * tiles + j, 0)),
            pl.BlockSpec((1, d), lambda i, j: (0, 0)),
            pl.BlockSpec((d, 3 * d), lambda i, j: (0, 0)),
            pl.BlockSpec((3, d), lambda i, j: (0, 0)),
            pl.BlockSpec((d, d), lambda i, j: (0, 0)),
        ],
        out_specs=pl.BlockSpec((rows, d), lambda i, j: (i * tiles + j, 0)),
        out_shape=jax.ShapeDtypeStruct((n, d), F32),
        scratch_shapes=[pltpu.VMEM((CONV_CARRY_ROWS, d), F32)],
        compiler_params=_params("arbitrary", "arbitrary"),
        name="conv_mixer",
    )(x, g.reshape(1, d), w_in, conv_w, w_out)


def _ffn_kernel(*refs, has_o):
    if has_o:
        x_ref, o_ref, wo_ref, g_ref, wg_ref, wu_ref, wd_ref, out_ref, h_ref = refs
    else:
        x_ref, g_ref, wg_ref, wu_ref, wd_ref, out_ref, h_ref = refs

    @pl.when(pl.program_id(1) == 0)
    def _():
        x = x_ref[...]
        if has_o:
            x = x + _dot(o_ref[...], wo_ref[...])
        out_ref[...] = x
        h_ref[...] = (_rms_unit(x) * g_ref[...]).astype(BF16)

    h = h_ref[...]
    gate = _dot(h, wg_ref[...])
    up = _dot(h, wu_ref[...])
    act = (gate * jax.nn.sigmoid(gate) * up).astype(BF16)
    out_ref[...] += _dot(act, wd_ref[...])


def _ff_tile(d_ff):
    half = d_ff // 2
    return half if half % V7X_LANES == 0 else d_ff


def _ffn(x, g, w_gu, w_down, layer, o=None, w_o=None):
    n, d = x.shape
    d_ff = w_down.shape[1]
    rows = min(FFN_ROWS, n)
    tf = _ff_tile(d_ff)
    n_f = d_ff // tf
    has_o = o is not None
    row_spec = pl.BlockSpec((rows, d), lambda i, f: (i, 0))
    in_specs = [row_spec]
    args = [x]
    if has_o:
        in_specs += [row_spec, pl.BlockSpec((d, d), lambda i, f: (0, 0))]
        args += [o, w_o]
    in_specs += [
        pl.BlockSpec((1, d), lambda i, f: (0, 0)),
        pl.BlockSpec((None, d, tf), lambda i, f: (layer, 0, f)),
        pl.BlockSpec((None, d, tf), lambda i, f: (layer, 0, f + n_f)),
        pl.BlockSpec((None, tf, d), lambda i, f: (layer, f, 0)),
    ]
    args += [g.reshape(1, d), w_gu, w_gu, w_down]
    return pl.pallas_call(
        functools.partial(_ffn_kernel, has_o=has_o),
        grid=(n // rows, n_f),
        in_specs=in_specs,
        out_specs=row_spec,
        out_shape=jax.ShapeDtypeStruct((n, d), F32),
        scratch_shapes=[pltpu.VMEM((rows, d), BF16)],
        compiler_params=_params("arbitrary", "arbitrary"),
        name="dense_swiglu",
    )(*args)


def _proj_kernel(*refs, has_y, has_kv, rows, d, q_scale):
    refs = list(refs)
    x_ref = refs.pop(0)
    if has_y:
        y0_ref = refs.pop(0)
        y1_ref = refs.pop(0)
    gq_ref = refs.pop(0)
    wq_ref = refs.pop(0)
    if has_kv:
        gkv_ref = refs.pop(0)
        wkv_ref = refs.pop(0)
    if has_y:
        xs_ref = refs.pop(0)
    q_ref = refs.pop(0)
    if has_kv:
        k_ref = refs.pop(0)
        v_ref = refs.pop(0)

    x = x_ref[...]
    if has_y:
        x = x + _load_token_major(y0_ref, rows) + _load_token_major(y1_ref, rows)
        xs_ref[...] = x
    xn = _rms_unit(x)
    q = _dot((xn * gq_ref[...]).astype(BF16), wq_ref[...])
    q_ref[...] = (q * q_scale).astype(BF16)
    if has_kv:
        kv = _dot((xn * gkv_ref[...]).astype(BF16), wkv_ref[...])
        k_ref[...] = kv[:, :d].astype(BF16)
        v_ref[...] = kv[:, d:].astype(BF16)


def _proj(x, g_q, w_q, q_scale, y=None, g_kv=None, w_kv=None):
    n, d = x.shape
    rows = min(PROJ_ROWS, n)
    has_y = y is not None
    has_kv = w_kv is not None
    row_spec = pl.BlockSpec((rows, d), lambda i: (i, 0))
    vec_spec = pl.BlockSpec((1, d), lambda i: (0, 0))
    in_specs = [row_spec]
    args = [x]
    if has_y:
        blocks_per_slot = n // rows
        tm_spec0 = pl.BlockSpec((rows * V7X_SUBLANES, V7X_LANES), lambda i: (i, 0))
        tm_spec1 = pl.BlockSpec((rows * V7X_SUBLANES, V7X_LANES),
                                lambda i: (i + blocks_per_slot, 0))
        in_specs += [tm_spec0, tm_spec1]
        args += [y, y]
    in_specs += [vec_spec, pl.BlockSpec((d, d), lambda i: (0, 0))]
    args += [g_q.reshape(1, d), w_q]
    if has_kv:
        in_specs += [vec_spec, pl.BlockSpec((d, 2 * d), lambda i: (0, 0))]
        args += [g_kv.reshape(1, d), w_kv]
    out_specs = []
    out_shape = []
    if has_y:
        out_specs.append(row_spec)
        out_shape.append(jax.ShapeDtypeStruct((n, d), F32))
    n_bf16 = 3 if has_kv else 1
    out_specs += [row_spec] * n_bf16
    out_shape += [jax.ShapeDtypeStruct((n, d), BF16)] * n_bf16
    return pl.pallas_call(
        functools.partial(_proj_kernel, has_y=has_y, has_kv=has_kv, rows=rows, d=d,
                          q_scale=q_scale),
        grid=(n // rows,),
        in_specs=in_specs,
        out_specs=out_specs,
        out_shape=out_shape,
        compiler_params=_params("arbitrary"),
        name="norm_qkv_proj",
    )(*args)


def _attn_kernel(q_ref, k_ref, v_ref, later_ref, o_ref, qs_ref, acc_ref, run_ref, *,
                 blk, dh, q_blocks):
    lane = lax.broadcasted_iota(jnp.int32, (1, HEADS_PER_STEP * dh), 1)
    head_mask = [(lane >= hh * dh) & (lane < (hh + 1) * dh) for hh in range(HEADS_PER_STEP)]
    stacked = HEADS_PER_STEP * blk

    def visit(key_blocks, diagonal_first):
        later = later_ref[...]
        qs = qs_ref[...]
        run = run_ref[...]
        total = None
        for pos, kb in enumerate(key_blocks):
            masked = diagonal_first and pos == 0
            start = pl.multiple_of(kb * blk, blk)
            k_blk = k_ref[pl.ds(start, blk), :]
            v_blk = v_ref[pl.ds(start, blk), :]
            z2 = lax.dot_general(qs, k_blk, (((1,), (1,)), ((), ())), preferred_element_type=F32)
            soft = jnp.log2(1.0 + jnp.exp2(-jnp.abs(z2)))
            log_beta = jnp.minimum(z2, 0.0) - soft
            log_1m_beta = log_beta - z2
            if masked:
                t_idx = lax.broadcasted_iota(jnp.int32, (stacked, blk), 0) & (blk - 1)
                s_idx = lax.broadcasted_iota(jnp.int32, (stacked, blk), 1)
                causal = s_idx < t_idx
                log_1m_beta = jnp.where(causal, log_1m_beta, 0.0)
            suffix = _dot(log_1m_beta.astype(BF16), later)
            attn = jnp.exp2(log_beta + suffix + run)
            if masked:
                attn = jnp.where(causal, attn, 0.0)
            av = _dot(attn.astype(BF16), v_blk)
            out = jnp.where(head_mask[0], av[:blk], av[blk:])
            total = out if total is None else total + out
            run = run + jnp.sum(log_1m_beta, axis=-1, keepdims=True)
        acc_ref[...] += total
        run_ref[...] = run

    def q_tile(qi, carry):
        q_rows = pl.ds(pl.multiple_of(qi * blk, blk), blk)
        q = q_ref[q_rows, :]
        for hh in range(HEADS_PER_STEP):
            qs_ref[hh * blk:(hh + 1) * blk, :] = jnp.where(head_mask[hh], q, jnp.zeros_like(q))
        acc_ref[...] = jnp.zeros_like(acc_ref)
        run_ref[...] = jnp.zeros_like(run_ref)

        lead = (qi % ATTN_GROUP) + 1
        for size in range(1, ATTN_GROUP + 1):
            @pl.when(lead == size)
            def _(size=size):
                visit([qi - p for p in range(size)], True)

        first = qi - lead

        def body(it, c):
            kb = first - ATTN_GROUP * it
            visit([kb - p for p in range(ATTN_GROUP)], False)
            return c

        lax.fori_loop(0, (first + 1) // ATTN_GROUP, body, 0)
        o_ref[q_rows, :] = acc_ref[...].astype(BF16)
        return carry

    lax.fori_loop(0, q_blocks, q_tile, 0)


def _attention(q, k, v, batch, seq):
    n, d = q.shape
    dh = d // N_HEADS
    blk = min(ATTN_BLOCK, seq)
    width = HEADS_PER_STEP * dh
    q_blocks = seq // blk
    idx = jnp.arange(blk, dtype=jnp.int32)
    later = (idx[:, None] > idx[None, :]).astype(BF16)
    col_blocks = d // width
    seq_spec = pl.BlockSpec((seq, width), lambda g: (g // col_blocks, g % col_blocks))
    return pl.pallas_call(
        functools.partial(_attn_kernel, blk=blk, dh=dh, q_blocks=q_blocks),
        grid=(batch * col_blocks,),
        in_specs=[seq_spec, seq_spec, seq_spec, pl.BlockSpec((blk, blk), lambda g: (0, 0))],
        out_specs=seq_spec,
        out_shape=jax.ShapeDtypeStruct((n, d), BF16),
        scratch_shapes=[pltpu.VMEM((HEADS_PER_STEP * blk, width), BF16),
                        pltpu.VMEM((blk, width), F32),
                        pltpu.VMEM((HEADS_PER_STEP * blk, 1), F32)],
        compiler_params=_params("arbitrary"),
        name="stick_breaking_attention",
    )(q, k, v, later)


def _router_kernel(*refs, has_o, rows):
    refs = list(refs)
    x_ref = refs.pop(0)
    if has_o:
        o_ref = refs.pop(0)
        wo_ref = refs.pop(0)
    g_ref = refs.pop(0)
    wr_hi_ref = refs.pop(0)
    wr_lo_ref = refs.pop(0)
    if has_o:
        xs_ref = refs.pop(0)
    h_ref, idx_ref, w_ref = refs

    x = x_ref[...]
    if has_o:
        x = x + _dot(o_ref[...], wo_ref[...])
        xs_ref[...] = x
    h = _rms_unit(x) * g_ref[...]
    _store_token_major(h_ref, h, rows)

    h_hi = h.astype(BF16)
    h_lo = (h - h_hi.astype(F32)).astype(BF16)
    logits = _dot(h_hi, wr_hi_ref[...]) + (_dot(h_lo, wr_hi_ref[...]) + _dot(h_hi, wr_lo_ref[...]))

    lane = lax.broadcasted_iota(jnp.int32, logits.shape, 1)
    lane_f = lane.astype(F32)
    neg_inf = jnp.float32(-jnp.inf)
    no_lane = jnp.float32(V7X_LANES)
    lg = jnp.where(lane < N_EXPERTS, logits, neg_inf)
    m1 = jnp.max(lg, axis=-1, keepdims=True)
    i1 = jnp.min(jnp.where(lg == m1, lane_f, no_lane), axis=-1, keepdims=True)
    lg2 = jnp.where(lane_f == i1, neg_inf, lg)
    m2 = jnp.max(lg2, axis=-1, keepdims=True)
    i2 = jnp.min(jnp.where(lg2 == m2, lane_f, no_lane), axis=-1, keepdims=True)
    e2 = jnp.exp(m2 - m1)
    w1 = 1.0 / (1.0 + e2)
    w2 = e2 * w1
    idx_ref[...] = jnp.where(lane == 0, i1, jnp.where(lane == 1, i2, 0.0)).astype(jnp.int32)
    w_ref[...] = jnp.where(lane == 0, w1, jnp.where(lane == 1, w2, 0.0))


def _router(x, g, w_router, o=None, w_o=None):
    n, d = x.shape
    rows = min(ROUTER_ROWS, n)
    has_o = o is not None
    wr = jnp.zeros((d, V7X_LANES), F32).at[:, :N_EXPERTS].set(w_router)
    wr_hi = wr.astype(BF16)
    wr_lo = (wr - wr_hi.astype(F32)).astype(BF16)
    row_spec = pl.BlockSpec((rows, d), lambda i: (i, 0))
    lane_spec = pl.BlockSpec((rows, V7X_LANES), lambda i: (i, 0))
    in_specs = [row_spec]
    args = [x]
    if has_o:
        in_specs += [row_spec, pl.BlockSpec((d, d), lambda i: (0, 0))]
        args += [o, w_o]
    in_specs += [pl.BlockSpec((1, d), lambda i: (0, 0)),
                 pl.BlockSpec((d, V7X_LANES), lambda i: (0, 0)),
                 pl.BlockSpec((d, V7X_LANES), lambda i: (0, 0))]
    args += [g.reshape(1, d), wr_hi, wr_lo]
    out_specs = []
    out_shape = []
    if has_o:
        out_specs.append(row_spec)
        out_shape.append(jax.ShapeDtypeStruct((n, d), F32))
    out_specs += [pl.BlockSpec((rows * V7X_SUBLANES, V7X_LANES), lambda i: (i, 0)),
                  lane_spec, lane_spec]
    out_shape += [jax.ShapeDtypeStruct((n * V7X_SUBLANES, V7X_LANES), F32),
                  jax.ShapeDtypeStruct((n, V7X_LANES), jnp.int32),
                  jax.ShapeDtypeStruct((n, V7X_LANES), F32)]
    outs = pl.pallas_call(
        functools.partial(_router_kernel, has_o=has_o, rows=rows),
        grid=(n // rows,),
        in_specs=in_specs,
        out_specs=out_specs,
        out_shape=out_shape,
        compiler_params=_params("arbitrary"),
        name="router_top2",
    )(*args)
    if has_o:
        xs, h_tm, idx, w = outs
    else:
        h_tm, idx, w = outs
        xs = x
    return xs, h_tm, idx[:, :TOP_K], w[:, :TOP_K]


def _plan_routing(idx, w, rows):
    n = idx.shape[0]
    n_pairs = n * TOP_K
    n_tiles = n_pairs // rows + N_EXPERTS
    e_flat = idx.reshape(n_pairs)
    pair_id = jnp.arange(n_pairs, dtype=jnp.int32)
    _, order, w_sorted = lax.sort((e_flat, pair_id, w.reshape(n_pairs)), num_keys=1, is_stable=True)
    experts = jnp.arange(N_EXPERTS, dtype=jnp.int32)
    counts = jnp.sum((e_flat[:, None] == experts[None, :]).astype(jnp.int32), axis=0)
    starts = jnp.cumsum(counts) - counts
    pad_counts = ((counts + rows - 1) // rows) * rows
    pad_ends = jnp.cumsum(pad_counts)
    pad_starts = pad_ends - pad_counts
    tile_start = jnp.arange(n_tiles, dtype=jnp.int32) * rows
    e_of_tile = jnp.sum((tile_start[:, None] >= pad_ends[None, :]).astype(jnp.int32), axis=1)
    tile_expert = jnp.minimum(e_of_tile, N_EXPERTS - 1)
    tile_within = tile_start - pad_starts[tile_expert]
    tile_count = jnp.where(e_of_tile < N_EXPERTS,
                           jnp.clip(counts[tile_expert] - tile_within, 0, rows), 0)
    lane = jnp.arange(rows, dtype=jnp.int32)
    sorted_pos = jnp.clip((starts[tile_expert] + tile_within)[:, None] + lane[None, :],
                          0, n_pairs - 1)
    valid = lane[None, :] < tile_count[:, None]
    pair = order[sorted_pos]
    token = pair // TOP_K
    slot = pair % TOP_K
    src = jnp.where(valid, token, 0)
    dst = jnp.where(valid, slot * n + token, 0)
    gate = jnp.where(valid, w_sorted[sorted_pos], 0.0)
    return (src.reshape(n_tiles, 1, rows), dst.reshape(n_tiles, 1, rows),
            gate.reshape(n_tiles * rows, 1), tile_expert, tile_count)


def _expert_kernel(te_ref, tc_ref, src_ref, src_next_ref, dst_ref, gate_ref, h_hbm,
                   wg_ref, wu_ref, wd_ref, y_hbm,
                   gbuf_ref, ybuf_ref, hbf_ref, acc_ref, gsem, ssem, *, rows, n_f, n_tiles):
    i = pl.program_id(0)
    j = pl.program_id(1)
    count = tc_ref[i]
    valid = count > 0
    full = count == rows
    first_step = j == 0
    last_step = j == n_f - 1
    has_next = (i + 1 < n_tiles) & (tc_ref[jnp.minimum(i + 1, n_tiles - 1)] > 0)
    prev_in_flight = (i > 0) & (tc_ref[jnp.maximum(i - 1, 0)] == rows)
    tile_rows = rows * V7X_SUBLANES

    def row_window(r):
        return pl.ds(pl.multiple_of(r * V7X_SUBLANES, V7X_SUBLANES), V7X_SUBLANES)

    def start_gather(idx_ref, r, slot):
        pltpu.make_async_copy(h_hbm.at[row_window(idx_ref[0, r]), :],
                              gbuf_ref.at[slot, row_window(r), :], gsem.at[slot]).start()

    def wait_gather(slot):
        pltpu.make_async_copy(h_hbm.at[pl.ds(0, tile_rows), :], gbuf_ref.at[slot],
                              gsem.at[slot]).wait()

    def scatter_copy(r):
        return pltpu.make_async_copy(ybuf_ref.at[row_window(r), :],
                                     y_hbm.at[row_window(dst_ref[0, r]), :], ssem.at[0])

    def wait_full_scatter():
        pltpu.make_async_copy(ybuf_ref, y_hbm.at[pl.ds(0, tile_rows), :], ssem.at[0]).wait()

    def for_all_rows(fn):
        def body(b, carry):
            for u in range(DMA_UNROLL):
                fn(b * DMA_UNROLL + u)
            return carry
        lax.fori_loop(0, rows // DMA_UNROLL, body, 0)

    def for_rows(n_rows, fn):
        def body(r, carry):
            fn(r)
            return carry
        lax.fori_loop(0, n_rows, body, 0)

    @pl.when(first_step & (i == 0))
    def _():
        for_all_rows(lambda r: start_gather(src_ref, r, 0))

    for slot in range(2):
        mine = valid & first_step & (i % 2 == slot)

        @pl.when(mine)
        def _(slot=slot):
            wait_gather(slot)
            hbf_ref[...] = _load_token_major(gbuf_ref.at[slot], rows).astype(BF16)
            acc_ref[...] = jnp.zeros_like(acc_ref)

        @pl.when(mine & has_next)
        def _(slot=slot):
            for_all_rows(lambda r: start_gather(src_next_ref, r, 1 - slot))

    @pl.when(valid)
    def _():
        h = hbf_ref[...]
        gate = _dot(h, wg_ref[...])
        up = _dot(h, wu_ref[...])
        act = (gate * jax.nn.sigmoid(gate) * up).astype(BF16)
        acc_ref[...] += _dot(act, wd_ref[...])

    @pl.when(valid & last_step & prev_in_flight)
    def _():
        wait_full_scatter()

    @pl.when(valid & last_step)
    def _():
        _store_token_major(ybuf_ref, acc_ref[...] * gate_ref[...], rows)

    @pl.when(valid & last_step & full)
    def _():
        for_all_rows(lambda r: scatter_copy(r).start())

    @pl.when(valid & last_step & full & jnp.logical_not(has_next))
    def _():
        wait_full_scatter()

    @pl.when(valid & last_step & jnp.logical_not(full))
    def _():
        for_rows(count, lambda r: scatter_copy(r).start())
        for_rows(count, lambda r: scatter_copy(r).wait())


def _experts(h_tm, idx, w, w_gu, w_down, layer, n):
    d = w_down.shape[3]
    d_ff = w_down.shape[2]
    rows = min(EXPERT_ROWS, n)
    assert rows % DMA_UNROLL == 0
    tf = _ff_tile(d_ff)
    n_f = d_ff // tf
    src, dst, gate, tile_expert, tile_count = _plan_routing(idx, w, rows)
    n_tiles = src.shape[0]

    def f_idx(i, j, tc):
        return jnp.where(tc[i] > 0, jnp.where(i % 2 == 0, j, n_f - 1 - j), 0)

    idx_spec = functools.partial(pl.BlockSpec, (None, 1, rows), memory_space=pltpu.SMEM)
    grid_spec = pltpu.PrefetchScalarGridSpec(
        num_scalar_prefetch=2,
        grid=(n_tiles, n_f),
        in_specs=[
            idx_spec(lambda i, j, te, tc: (i, 0, 0)),
            idx_spec(lambda i, j, te, tc: (jnp.minimum(i + 1, n_tiles - 1), 0, 0)),
            idx_spec(lambda i, j, te, tc: (i, 0, 0)),
            pl.BlockSpec((rows, 1), lambda i, j, te, tc: (i, 0)),
            pl.BlockSpec(memory_space=pl.ANY),
            pl.BlockSpec((None, None, d, tf),
                         lambda i, j, te, tc: (layer, te[i], 0, f_idx(i, j, tc))),
            pl.BlockSpec((None, None, d, tf),
                         lambda i, j, te, tc: (layer, te[i], 0, f_idx(i, j, tc) + n_f)),
            pl.BlockSpec((None, None, tf, d),
                         lambda i, j, te, tc: (layer, te[i], f_idx(i, j, tc), 0)),
        ],
        out_specs=pl.BlockSpec(memory_space=pl.ANY),
        scratch_shapes=[
            pltpu.VMEM((2, rows * V7X_SUBLANES, V7X_LANES), F32),
            pltpu.VMEM((rows * V7X_SUBLANES, V7X_LANES), F32),
            pltpu.VMEM((rows, d), BF16),
            pltpu.VMEM((rows, d), F32),
            pltpu.SemaphoreType.DMA((2,)),
            pltpu.SemaphoreType.DMA((1,)),
        ],
    )
    return pl.pallas_call(
        functools.partial(_expert_kernel, rows=rows, n_f=n_f, n_tiles=n_tiles),
        grid_spec=grid_spec,
        out_shape=jax.ShapeDtypeStruct((n * TOP_K * V7X_SUBLANES, V7X_LANES), F32),
        compiler_params=_params("arbitrary", "arbitrary"),
        name="routed_swiglu",
    )(tile_expert, tile_count, src, src, dst, gate, h_tm, w_gu, w_gu, w_down)


def _final_kernel(x_ref, y0_ref, y1_ref, g_ref, out_ref, *, rows):
    x = x_ref[...] + _load_token_major(y0_ref, rows) + _load_token_major(y1_ref, rows)
    out_ref[...] = _rms_unit(x) * g_ref[...]


def _final_norm(x, y, g):
    n, d = x.shape
    rows = min(PROJ_ROWS, n)
    blocks_per_slot = n // rows
    row_spec = pl.BlockSpec((rows, d), lambda i: (i, 0))
    return pl.pallas_call(
        functools.partial(_final_kernel, rows=rows),
        grid=(n // rows,),
        in_specs=[
            row_spec,
            pl.BlockSpec((rows * V7X_SUBLANES, V7X_LANES), lambda i: (i, 0)),
            pl.BlockSpec((rows * V7X_SUBLANES, V7X_LANES), lambda i: (i + blocks_per_slot, 0)),
            pl.BlockSpec((1, d), lambda i: (0, 0)),
        ],
        out_specs=row_spec,
        out_shape=jax.ShapeDtypeStruct((n, d), F32),
        compiler_params=_params("arbitrary"),
        name="final_norm",
    )(x, y, y, g.reshape(1, d))


def kernel(x, g_mix, g_ffn, g_final, a_w_in, a_conv_w, a_w_out, g_kv, w_kv, b_w_q, b_w_o,
           ffn_w_gu, ffn_w_down, moe_w_router, moe_w_gu, moe_w_down):
    batch, seq, d = x.shape
    n = batch * seq
    depth = g_mix.shape[0]
    n_self = a_w_in.shape[0]
    q_scale = LOG2_E / math.sqrt(d // N_HEADS)

    def bf(a):
        return a.astype(BF16)

    ffn_w_gu, ffn_w_down, moe_w_gu, moe_w_down = map(bf, (ffn_w_gu, ffn_w_down, moe_w_gu, moe_w_down))

    xs = x.reshape(n, d)
    y = None
    k = v = None
    for i in range(depth):
        o = w_o = None
        if i < n_self:
            assert y is None
            xs = _mixer(xs, g_mix[i], bf(a_w_in[i]), a_conv_w[i], bf(a_w_out[i]), batch, seq)
        else:
            j = i - n_self
            first = i == n_self
            outs = _proj(xs, g_mix[i], bf(b_w_q[j]), q_scale, y=y,
                         g_kv=g_kv if first else None, w_kv=bf(w_kv) if first else None)
            outs = list(outs)
            if y is not None:
                xs = outs.pop(0)
                y = None
            q = outs.pop(0)
            if first:
                k, v = outs
            o = _attention(q, k, v, batch, seq)
            w_o = bf(b_w_o[j])
        assert y is None
        if i % 2 == 0:
            xs = _ffn(xs, g_ffn[i], ffn_w_gu, ffn_w_down, i // 2, o=o, w_o=w_o)
        else:
            m = i // 2
            xs, h_tm, idx, w = _router(xs, g_ffn[i], moe_w_router[m], o=o, w_o=w_o)
            y = _experts(h_tm, idx, w, moe_w_gu, moe_w_down, m, n)
    assert y is not None
    return _final_norm(xs, y, g_final).reshape(batch, seq, d)
```

```python
import functools
import math

import jax
import jax.numpy as jnp
from jax import lax
from jax.experimental import pallas as pl
from jax.experimental.pallas import tpu as pltpu

F32 = jnp.float32
BF16 = jnp.bfloat16

RMS_EPS = 1e-6
LOG2_E = math.log2(math.e)
N_HEADS = 16
N_EXPERTS = 8
TOP_K = 2

V7X_LANES = 128
V7X_SUBLANES = 8
V7X_VMEM_BYTES = 64 * 1024 * 1024
VMEM_LIMIT = V7X_VMEM_BYTES - 8 * 1024 * 1024

MIXER_ROWS = 512
FFN_ROWS = 512
PROJ_ROWS = 1024
ROUTER_ROWS = 512
EXPERT_ROWS = 512
ATTN_BLOCK = 256
ATTN_GROUP = 4
HEADS_PER_STEP = 2
CONV_CARRY_ROWS = V7X_SUBLANES
DMA_UNROLL = 64
EXPERT_FF_CHUNKS = 2


def _params(*semantics):
    return pltpu.CompilerParams(dimension_semantics=semantics,
                                vmem_limit_bytes=VMEM_LIMIT)


def _dot(a, b):
    return jnp.dot(a, b, preferred_element_type=F32)


def _rms_unit(x):
    return x * lax.rsqrt(jnp.mean(x * x, axis=-1, keepdims=True) + RMS_EPS)


def _load_token_major(ref, rows):
    parts = [ref[pl.ds(j, rows, stride=V7X_SUBLANES), :] for j in range(V7X_SUBLANES)]
    return jnp.concatenate(parts, axis=1)


def _store_token_major(ref, val, rows):
    for j in range(V7X_SUBLANES):
        ref[pl.ds(j, rows, stride=V7X_SUBLANES), :] = val[:, j * V7X_LANES:(j + 1) * V7X_LANES]


def _mixer_kernel(x_ref, g_ref, win_ref, cw_ref, wout_ref, out_ref, carry_ref, *, rows, d):
    s = pl.program_id(1)

    @pl.when(s == 0)
    def _():
        carry_ref[...] = jnp.zeros_like(carry_ref)

    x = x_ref[...]
    h = (_rms_unit(x) * g_ref[...]).astype(BF16)
    proj = _dot(h, win_ref[...])
    u = proj[:, :d] * proj[:, d:2 * d]
    b_gate = proj[:, 2 * d:]
    prev = carry_ref[...]
    row = lax.broadcasted_iota(jnp.int32, (rows, d), 0)
    u1 = jnp.where(row == 0, prev[CONV_CARRY_ROWS - 1:CONV_CARRY_ROWS, :], pltpu.roll(u, 1, 0))
    u2 = pltpu.roll(u, 2, 0)
    u2 = jnp.where(row == 0, prev[CONV_CARRY_ROWS - 2:CONV_CARRY_ROWS - 1, :], u2)
    u2 = jnp.where(row == 1, prev[CONV_CARRY_ROWS - 1:CONV_CARRY_ROWS, :], u2)
    carry_ref[...] = u[rows - CONV_CARRY_ROWS:, :]
    cw = cw_ref[...]
    conv = cw[0:1, :] * u2 + cw[1:2, :] * u1 + cw[2:3, :] * u
    mixed = (b_gate * conv).astype(BF16)
    out_ref[...] = x + _dot(mixed, wout_ref[...])


def _mixer(x, g, w_in, conv_w, w_out, batch, seq):
    n, d = x.shape
    rows = min(MIXER_ROWS, seq)
    tiles = seq // rows
    return pl.pallas_call(
        functools.partial(_mixer_kernel, rows=rows, d=d),
        grid=(batch, tiles),
        in_specs=[
            pl.BlockSpec((rows, d), lambda i, j: (i * tiles + j, 0)),
            pl.BlockSpec((1, d), lambda i, j: (0, 0)),
            pl.BlockSpec((d, 3 * d), lambda i, j: (0, 0)),
            pl.BlockSpec((3, d), lambda i, j: (0, 0)),
            pl.BlockSpec((d, d), lambda i, j: (0, 0)),
        ],
        out_specs=pl.BlockSpec((rows, d), lambda i, j: (i * tiles + j, 0)),
        out_shape=jax.ShapeDtypeStruct((n, d), F32),
        scratch_shapes=[pltpu.VMEM((CONV_CARRY_ROWS, d), F32)],
        compiler_params=_params("arbitrary", "arbitrary"),
        name="conv_mixer",
    )(x, g.reshape(1, d), w_in, conv_w, w_out)


def _ffn_kernel(*refs, has_o):
    if has_o:
        x_ref, o_ref, wo_ref, g_ref, wg_ref, wu_ref, wd_ref, out_ref, h_ref = refs
    else:
        x_ref, g_ref, wg_ref, wu_ref, wd_ref, out_ref, h_ref = refs

    @pl.when(pl.program_id(1) == 0)
    def _():
        x = x_ref[...]
        if has_o:
            x = x + _dot(o_ref[...], wo_ref[...])
        out_ref[...] = x
        h_ref[...] = (_rms_unit(x) * g_ref[...]).astype(BF16)

    h = h_ref[...]
    gate = _dot(h, wg_ref[...])
    up = _dot(h, wu_ref[...])
    act = (gate * jax.nn.sigmoid(gate) * up).astype(BF16)
    out_ref[...] += _dot(act, wd_ref[...])


def _ff_tile(d_ff):
    half = d_ff // 2
    return half if half % V7X_LANES == 0 else d_ff


def _lane_chunks(width, n_chunks):
    lane_groups = width // V7X_LANES
    bounds = [(lane_groups * c // n_chunks) * V7X_LANES for c in range(n_chunks + 1)]
    return [(a, b) for a, b in zip(bounds[:-1], bounds[1:]) if b > a]


def _ffn(x, g, w_gu, w_down, layer, o=None, w_o=None):
    n, d = x.shape
    d_ff = w_down.shape[1]
    rows = min(FFN_ROWS, n)
    tf = _ff_tile(d_ff)
    n_f = d_ff // tf
    has_o = o is not None
    row_spec = pl.BlockSpec((rows, d), lambda i, f: (i, 0))
    in_specs = [row_spec]
    args = [x]
    if has_o:
        in_specs += [row_spec, pl.BlockSpec((d, d), lambda i, f: (0, 0))]
        args += [o, w_o]
    in_specs += [
        pl.BlockSpec((1, d), lambda i, f: (0, 0)),
        pl.BlockSpec((None, d, tf), lambda i, f: (layer, 0, f)),
        pl.BlockSpec((None, d, tf), lambda i, f: (layer, 0, f + n_f)),
        pl.BlockSpec((None, tf, d), lambda i, f: (layer, f, 0)),
    ]
    args += [g.reshape(1, d), w_gu, w_gu, w_down]
    return pl.pallas_call(
        functools.partial(_ffn_kernel, has_o=has_o),
        grid=(n // rows, n_f),
        in_specs=in_specs,
        out_specs=row_spec,
        out_shape=jax.ShapeDtypeStruct((n, d), F32),
        scratch_shapes=[pltpu.VMEM((rows, d), BF16)],
        compiler_params=_params("arbitrary", "arbitrary"),
        name="dense_swiglu",
    )(*args)


def _proj_kernel(*refs, has_y, has_kv, rows, d, q_scale):
    refs = list(refs)
    x_ref = refs.pop(0)
    if has_y:
        y0_ref = refs.pop(0)
        y1_ref = refs.pop(0)
    gq_ref = refs.pop(0)
    wq_ref = refs.pop(0)
    if has_kv:
        gkv_ref = refs.pop(0)
        wkv_ref = refs.pop(0)
    if has_y:
        xs_ref = refs.pop(0)
    q_ref = refs.pop(0)
    if has_kv:
        k_ref = refs.pop(0)
        v_ref = refs.pop(0)

    x = x_ref[...]
    if has_y:
        x = x + _load_token_major(y0_ref, rows) + _load_token_major(y1_ref, rows)
        xs_ref[...] = x
    xn = _rms_unit(x)
    q = _dot((xn * gq_ref[...]).astype(BF16), wq_ref[...])
    q_ref[...] = (q * q_scale).astype(BF16)
    if has_kv:
        kv = _dot((xn * gkv_ref[...]).astype(BF16), wkv_ref[...])
        k_ref[...] = kv[:, :d].astype(BF16)
        v_ref[...] = kv[:, d:].astype(BF16)


def _proj(x, g_q, w_q, q_scale, y=None, g_kv=None, w_kv=None):
    n, d = x.shape
    rows = min(PROJ_ROWS, n)
    has_y = y is not None
    has_kv = w_kv is not None
    row_spec = pl.BlockSpec((rows, d), lambda i: (i, 0))
    vec_spec = pl.BlockSpec((1, d), lambda i: (0, 0))
    in_specs = [row_spec]
    args = [x]
    if has_y:
        blocks_per_slot = n // rows
        tm_spec0 = pl.BlockSpec((rows * V7X_SUBLANES, V7X_LANES), lambda i: (i, 0))
        tm_spec1 = pl.BlockSpec((rows * V7X_SUBLANES, V7X_LANES),
                                lambda i: (i + blocks_per_slot, 0))
        in_specs += [tm_spec0, tm_spec1]
        args += [y, y]
    in_specs += [vec_spec, pl.BlockSpec((d, d), lambda i: (0, 0))]
    args += [g_q.reshape(1, d), w_q]
    if has_kv:
        in_specs += [vec_spec, pl.BlockSpec((d, 2 * d), lambda i: (0, 0))]
        args += [g_kv.reshape(1, d), w_kv]
    out_specs = []
    out_shape = []
    if has_y:
        out_specs.append(row_spec)
        out_shape.append(jax.ShapeDtypeStruct((n, d), F32))
    n_bf16 = 3 if has_kv else 1
    out_specs += [row_spec] * n_bf16
    out_shape += [jax.ShapeDtypeStruct((n, d), BF16)] * n_bf16
    return pl.pallas_call(
        functools.partial(_proj_kernel, has_y=has_y, has_kv=has_kv, rows=rows, d=d,
                          q_scale=q_scale),
        grid=(n // rows,),
        in_specs=in_specs,
        out_specs=out_specs,
        out_shape=out_shape,
        compiler_params=_params("arbitrary"),
        name="norm_qkv_proj",
    )(*args)


def _attn_kernel(q_ref, k_ref, v_ref, later_ref, o_ref, qs_ref, acc_ref, run_ref, *,
                 blk, dh, q_blocks):
    lane = lax.broadcasted_iota(jnp.int32, (1, HEADS_PER_STEP * dh), 1)
    head_mask = [(lane >= hh * dh) & (lane < (hh + 1) * dh) for hh in range(HEADS_PER_STEP)]
    stacked = HEADS_PER_STEP * blk

    def visit(key_blocks, diagonal_first):
        later = later_ref[...]
        qs = qs_ref[...]
        run = run_ref[...]
        total = None
        for pos, kb in enumerate(key_blocks):
            masked = diagonal_first and pos == 0
            start = pl.multiple_of(kb * blk, blk)
            k_blk = k_ref[pl.ds(start, blk), :]
            v_blk = v_ref[pl.ds(start, blk), :]
            z2 = lax.dot_general(qs, k_blk, (((1,), (1,)), ((), ())), preferred_element_type=F32)
            soft = jnp.log2(1.0 + jnp.exp2(-jnp.abs(z2)))
            log_beta = jnp.minimum(z2, 0.0) - soft
            log_1m_beta = log_beta - z2
            if masked:
                t_idx = lax.broadcasted_iota(jnp.int32, (stacked, blk), 0) & (blk - 1)
                s_idx = lax.broadcasted_iota(jnp.int32, (stacked, blk), 1)
                causal = s_idx < t_idx
                log_1m_beta = jnp.where(causal, log_1m_beta, 0.0)
            suffix = _dot(log_1m_beta.astype(BF16), later)
            attn = jnp.exp2(log_beta + suffix + run)
            if masked:
                attn = jnp.where(causal, attn, 0.0)
            av = _dot(attn.astype(BF16), v_blk)
            out = jnp.where(head_mask[0], av[:blk], av[blk:])
            total = out if total is None else total + out
            run = run + jnp.sum(log_1m_beta, axis=-1, keepdims=True)
        acc_ref[...] += total
        run_ref[...] = run

    def q_tile(qi, carry):
        q_rows = pl.ds(pl.multiple_of(qi * blk, blk), blk)
        q = q_ref[q_rows, :]
        for hh in range(HEADS_PER_STEP):
            qs_ref[hh * blk:(hh + 1) * blk, :] = jnp.where(head_mask[hh], q, jnp.zeros_like(q))
        acc_ref[...] = jnp.zeros_like(acc_ref)
        run_ref[...] = jnp.zeros_like(run_ref)

        lead = (qi % ATTN_GROUP) + 1
        for size in range(1, ATTN_GROUP + 1):
            @pl.when(lead == size)
            def _(size=size):
                visit([qi - p for p in range(size)], True)

        first = qi - lead

        def body(it, c):
            kb = first - ATTN_GROUP * it
            visit([kb - p for p in range(ATTN_GROUP)], False)
            return c

        lax.fori_loop(0, (first + 1) // ATTN_GROUP, body, 0)
        o_ref[q_rows, :] = acc_ref[...].astype(BF16)
        return carry

    lax.fori_loop(0, q_blocks, q_tile, 0)


def _attention(q, k, v, batch, seq):
    n, d = q.shape
    dh = d // N_HEADS
    blk = min(ATTN_BLOCK, seq)
    width = HEADS_PER_STEP * dh
    q_blocks = seq // blk
    idx = jnp.arange(blk, dtype=jnp.int32)
    later = (idx[:, None] > idx[None, :]).astype(BF16)
    col_blocks = d // width
    seq_spec = pl.BlockSpec((seq, width), lambda g: (g // col_blocks, g % col_blocks))
    return pl.pallas_call(
        functools.partial(_attn_kernel, blk=blk, dh=dh, q_blocks=q_blocks),
        grid=(batch * col_blocks,),
        in_specs=[seq_spec, seq_spec, seq_spec, pl.BlockSpec((blk, blk), lambda g: (0, 0))],
        out_specs=seq_spec,
        out_shape=jax.ShapeDtypeStruct((n, d), BF16),
        scratch_shapes=[pltpu.VMEM((HEADS_PER_STEP * blk, width), BF16),
                        pltpu.VMEM((blk, width), F32),
                        pltpu.VMEM((HEADS_PER_STEP * blk, 1), F32)],
        compiler_params=_params("arbitrary"),
        name="stick_breaking_attention",
    )(q, k, v, later)


def _router_kernel(*refs, has_o, rows):
    refs = list(refs)
    x_ref = refs.pop(0)
    if has_o:
        o_ref = refs.pop(0)
        wo_ref = refs.pop(0)
    g_ref = refs.pop(0)
    wr_hi_ref = refs.pop(0)
    wr_lo_ref = refs.pop(0)
    if has_o:
        xs_ref = refs.pop(0)
    h_ref, idx_ref, w_ref = refs

    x = x_ref[...]
    if has_o:
        x = x + _dot(o_ref[...], wo_ref[...])
        xs_ref[...] = x
    h = _rms_unit(x) * g_ref[...]
    _store_token_major(h_ref, h, rows)

    h_hi = h.astype(BF16)
    h_lo = (h - h_hi.astype(F32)).astype(BF16)
    logits = _dot(h_hi, wr_hi_ref[...]) + (_dot(h_lo, wr_hi_ref[...]) + _dot(h_hi, wr_lo_ref[...]))

    lane = lax.broadcasted_iota(jnp.int32, logits.shape, 1)
    lane_f = lane.astype(F32)
    neg_inf = jnp.float32(-jnp.inf)
    no_lane = jnp.float32(V7X_LANES)
    lg = jnp.where(lane < N_EXPERTS, logits, neg_inf)
    m1 = jnp.max(lg, axis=-1, keepdims=True)
    i1 = jnp.min(jnp.where(lg == m1, lane_f, no_lane), axis=-1, keepdims=True)
    lg2 = jnp.where(lane_f == i1, neg_inf, lg)
    m2 = jnp.max(lg2, axis=-1, keepdims=True)
    i2 = jnp.min(jnp.where(lg2 == m2, lane_f, no_lane), axis=-1, keepdims=True)
    e2 = jnp.exp(m2 - m1)
    w1 = 1.0 / (1.0 + e2)
    w2 = e2 * w1
    idx_ref[...] = jnp.where(lane == 0, i1, jnp.where(lane == 1, i2, 0.0)).astype(jnp.int32)
    w_ref[...] = jnp.where(lane == 0, w1, jnp.where(lane == 1, w2, 0.0))


def _router(x, g, w_router, o=None, w_o=None):
    n, d = x.shape
    rows = min(ROUTER_ROWS, n)
    has_o = o is not None
    wr = jnp.zeros((d, V7X_LANES), F32).at[:, :N_EXPERTS].set(w_router)
    wr_hi = wr.astype(BF16)
    wr_lo = (wr - wr_hi.astype(F32)).astype(BF16)
    row_spec = pl.BlockSpec((rows, d), lambda i: (i, 0))
    lane_spec = pl.BlockSpec((rows, V7X_LANES), lambda i: (i, 0))
    in_specs = [row_spec]
    args = [x]
    if has_o:
        in_specs += [row_spec, pl.BlockSpec((d, d), lambda i: (0, 0))]
        args += [o, w_o]
    in_specs += [pl.BlockSpec((1, d), lambda i: (0, 0)),
                 pl.BlockSpec((d, V7X_LANES), lambda i: (0, 0)),
                 pl.BlockSpec((d, V7X_LANES), lambda i: (0, 0))]
    args += [g.reshape(1, d), wr_hi, wr_lo]
    out_specs = []
    out_shape = []
    if has_o:
        out_specs.append(row_spec)
        out_shape.append(jax.ShapeDtypeStruct((n, d), F32))
    out_specs += [pl.BlockSpec((rows * V7X_SUBLANES, V7X_LANES), lambda i: (i, 0)),
                  lane_spec, lane_spec]
    out_shape += [jax.ShapeDtypeStruct((n * V7X_SUBLANES, V7X_LANES), F32),
                  jax.ShapeDtypeStruct((n, V7X_LANES), jnp.int32),
                  jax.ShapeDtypeStruct((n, V7X_LANES), F32)]
    outs = pl.pallas_call(
        functools.partial(_router_kernel, has_o=has_o, rows=rows),
        grid=(n // rows,),
        in_specs=in_specs,
        out_specs=out_specs,
        out_shape=out_shape,
        compiler_params=_params("arbitrary"),
        name="router_top2",
    )(*args)
    if has_o:
        xs, h_tm, idx, w = outs
    else:
        h_tm, idx, w = outs
        xs = x
    return xs, h_tm, idx[:, :TOP_K], w[:, :TOP_K]


def _plan_routing(idx, w, rows):
    n = idx.shape[0]
    n_pairs = n * TOP_K
    n_tiles = n_pairs // rows + N_EXPERTS
    e_flat = idx.reshape(n_pairs)
    pair_id = jnp.arange(n_pairs, dtype=jnp.int32)
    _, order, w_sorted = lax.sort((e_flat, pair_id, w.reshape(n_pairs)), num_keys=1, is_stable=True)
    experts = jnp.arange(N_EXPERTS, dtype=jnp.int32)
    counts = jnp.sum((e_flat[:, None] == experts[None, :]).astype(jnp.int32), axis=0)
    starts = jnp.cumsum(counts) - counts
    pad_counts = ((counts + rows - 1) // rows) * rows
    pad_ends = jnp.cumsum(pad_counts)
    pad_starts = pad_ends - pad_counts
    tile_start = jnp.arange(n_tiles, dtype=jnp.int32) * rows
    e_of_tile = jnp.sum((tile_start[:, None] >= pad_ends[None, :]).astype(jnp.int32), axis=1)
    tile_expert = jnp.minimum(e_of_tile, N_EXPERTS - 1)
    tile_within = tile_start - pad_starts[tile_expert]
    tile_count = jnp.where(e_of_tile < N_EXPERTS,
                           jnp.clip(counts[tile_expert] - tile_within, 0, rows), 0)
    lane = jnp.arange(rows, dtype=jnp.int32)
    sorted_pos = jnp.clip((starts[tile_expert] + tile_within)[:, None] + lane[None, :],
                          0, n_pairs - 1)
    valid = lane[None, :] < tile_count[:, None]
    pair = order[sorted_pos]
    token = pair // TOP_K
    slot = pair % TOP_K
    src = jnp.where(valid, token, 0)
    dst = jnp.where(valid, slot * n + token, 0)
    gate = jnp.where(valid, w_sorted[sorted_pos], 0.0)
    return (src.reshape(n_tiles, 1, rows), dst.reshape(n_tiles, 1, rows),
            gate.reshape(n_tiles * rows, 1), tile_expert, tile_count)


def _expert_kernel(te_ref, tc_ref, src_ref, src_next_ref, dst_ref, gate_ref, h_hbm,
                   wg_ref, wu_ref, wd_ref, y_hbm,
                   gbuf_ref, ybuf_ref, hbf_ref, acc_ref, gsem, ssem, *, rows, n_f, n_tiles):
    i = pl.program_id(0)
    j = pl.program_id(1)
    count = tc_ref[i]
    valid = count > 0
    full = count == rows
    first_step = j == 0
    last_step = j == n_f - 1
    has_next = (i + 1 < n_tiles) & (tc_ref[jnp.minimum(i + 1, n_tiles - 1)] > 0)
    prev_in_flight = (i > 0) & (tc_ref[jnp.maximum(i - 1, 0)] == rows)
    tile_rows = rows * V7X_SUBLANES

    def row_window(r):
        return pl.ds(pl.multiple_of(r * V7X_SUBLANES, V7X_SUBLANES), V7X_SUBLANES)

    def start_gather(idx_ref, r, slot):
        pltpu.make_async_copy(h_hbm.at[row_window(idx_ref[0, r]), :],
                              gbuf_ref.at[slot, row_window(r), :], gsem.at[slot]).start()

    def wait_gather(slot):
        pltpu.make_async_copy(h_hbm.at[pl.ds(0, tile_rows), :], gbuf_ref.at[slot],
                              gsem.at[slot]).wait()

    def scatter_copy(r):
        return pltpu.make_async_copy(ybuf_ref.at[row_window(r), :],
                                     y_hbm.at[row_window(dst_ref[0, r]), :], ssem.at[0])

    def wait_full_scatter():
        pltpu.make_async_copy(ybuf_ref, y_hbm.at[pl.ds(0, tile_rows), :], ssem.at[0]).wait()

    def for_all_rows(fn):
        def body(b, carry):
            for u in range(DMA_UNROLL):
                fn(b * DMA_UNROLL + u)
            return carry
        lax.fori_loop(0, rows // DMA_UNROLL, body, 0)

    def for_rows(n_rows, fn):
        def body(r, carry):
            fn(r)
            return carry
        lax.fori_loop(0, n_rows, body, 0)

    @pl.when(first_step & (i == 0))
    def _():
        for_all_rows(lambda r: start_gather(src_ref, r, 0))

    for slot in range(2):
        mine = valid & first_step & (i % 2 == slot)

        @pl.when(mine)
        def _(slot=slot):
            wait_gather(slot)
            hbf_ref[...] = _load_token_major(gbuf_ref.at[slot], rows).astype(BF16)
            acc_ref[...] = jnp.zeros_like(acc_ref)

        @pl.when(mine & has_next)
        def _(slot=slot):
            for_all_rows(lambda r: start_gather(src_next_ref, r, 1 - slot))

    @pl.when(valid)
    def _():
        h = hbf_ref[...]
        part = None
        for c0, c1 in _lane_chunks(wg_ref.shape[1], EXPERT_FF_CHUNKS):
            gate = _dot(h, wg_ref[:, c0:c1].astype(BF16))
            up = _dot(h, wu_ref[:, c0:c1].astype(BF16))
            act = (gate * jax.nn.sigmoid(gate) * up).astype(BF16)
            contrib = _dot(act, wd_ref[c0:c1, :].astype(BF16))
            part = contrib if part is None else part + contrib
        acc_ref[...] += part

    @pl.when(valid & last_step & prev_in_flight)
    def _():
        wait_full_scatter()

    @pl.when(valid & last_step)
    def _():
        _store_token_major(ybuf_ref, acc_ref[...] * gate_ref[...], rows)

    @pl.when(valid & last_step & full)
    def _():
        for_all_rows(lambda r: scatter_copy(r).start())

    @pl.when(valid & last_step & full & jnp.logical_not(has_next))
    def _():
        wait_full_scatter()

    @pl.when(valid & last_step & jnp.logical_not(full))
    def _():
        for_rows(count, lambda r: scatter_copy(r).start())
        for_rows(count, lambda r: scatter_copy(r).wait())


def _experts(h_tm, idx, w, w_gu, w_down, layer, n):
    d = w_down.shape[3]
    d_ff = w_down.shape[2]
    rows = min(EXPERT_ROWS, n)
    assert rows % DMA_UNROLL == 0
    tf = _ff_tile(d_ff)
    n_f = d_ff // tf
    src, dst, gate, tile_expert, tile_count = _plan_routing(idx, w, rows)
    n_tiles = src.shape[0]

    def f_idx(i, j, tc):
        return jnp.where(tc[i] > 0, jnp.where(i % 2 == 0, j, n_f - 1 - j), 0)

    idx_spec = functools.partial(pl.BlockSpec, (None, 1, rows), memory_space=pltpu.SMEM)
    grid_spec = pltpu.PrefetchScalarGridSpec(
        num_scalar_prefetch=2,
        grid=(n_tiles, n_f),
        in_specs=[
            idx_spec(lambda i, j, te, tc: (i, 0, 0)),
            idx_spec(lambda i, j, te, tc: (jnp.minimum(i + 1, n_tiles - 1), 0, 0)),
            idx_spec(lambda i, j, te, tc: (i, 0, 0)),
            pl.BlockSpec((rows, 1), lambda i, j, te, tc: (i, 0)),
            pl.BlockSpec(memory_space=pl.ANY),
            pl.BlockSpec((None, None, d, tf),
                         lambda i, j, te, tc: (layer, te[i], 0, f_idx(i, j, tc))),
            pl.BlockSpec((None, None, d, tf),
                         lambda i, j, te, tc: (layer, te[i], 0, f_idx(i, j, tc) + n_f)),
            pl.BlockSpec((None, None, tf, d),
                         lambda i, j, te, tc: (layer, te[i], f_idx(i, j, tc), 0)),
        ],
        out_specs=pl.BlockSpec(memory_space=pl.ANY),
        scratch_shapes=[
            pltpu.VMEM((2, rows * V7X_SUBLANES, V7X_LANES), F32),
            pltpu.VMEM((rows * V7X_SUBLANES, V7X_LANES), F32),
            pltpu.VMEM((rows, d), BF16),
            pltpu.VMEM((rows, d), F32),
            pltpu.SemaphoreType.DMA((2,)),
            pltpu.SemaphoreType.DMA((1,)),
        ],
    )
    return pl.pallas_call(
        functools.partial(_expert_kernel, rows=rows, n_f=n_f, n_tiles=n_tiles),
        grid_spec=grid_spec,
        out_shape=jax.ShapeDtypeStruct((n * TOP_K * V7X_SUBLANES, V7X_LANES), F32),
        compiler_params=_params("arbitrary", "arbitrary"),
        name="routed_swiglu",
    )(tile_expert, tile_count, src, src, dst, gate, h_tm, w_gu, w_gu, w_down)


def _final_kernel(x_ref, y0_ref, y1_ref, g_ref, out_ref, *, rows):
    x = x_ref[...] + _load_token_major(y0_ref, rows) + _load_token_major(y1_ref, rows)
    out_ref[...] = _rms_unit(x) * g_ref[...]


def _final_norm(x, y, g):
    n, d = x.shape
    rows = min(PROJ_ROWS, n)
    blocks_per_slot = n // rows
    row_spec = pl.BlockSpec((rows, d), lambda i: (i, 0))
    return pl.pallas_call(
        functools.partial(_final_kernel, rows=rows),
        grid=(n // rows,),
        in_specs=[
            row_spec,
            pl.BlockSpec((rows * V7X_SUBLANES, V7X_LANES), lambda i: (i, 0)),
            pl.BlockSpec((rows * V7X_SUBLANES, V7X_LANES), lambda i: (i + blocks_per_slot, 0)),
            pl.BlockSpec((1, d), lambda i: (0, 0)),
        ],
        out_specs=row_spec,
        out_shape=jax.ShapeDtypeStruct((n, d), F32),
        compiler_params=_params("arbitrary"),
        name="final_norm",
    )(x, y, y, g.reshape(1, d))


def kernel(x, g_mix, g_ffn, g_final, a_w_in, a_conv_w, a_w_out, g_kv, w_kv, b_w_q, b_w_o,
           ffn_w_gu, ffn_w_down, moe_w_router, moe_w_gu, moe_w_down):
    batch, seq, d = x.shape
    n = batch * seq
    depth = g_mix.shape[0]
    n_self = a_w_in.shape[0]
    q_scale = LOG2_E / math.sqrt(d // N_HEADS)

    def bf(a):
        return a.astype(BF16)

    ffn_w_gu, ffn_w_down = bf(ffn_w_gu), bf(ffn_w_down)

    xs = x.reshape(n, d)
    y = None
    k = v = None
    for i in range(depth):
        o = w_o = None
        if i < n_self:
            assert y is None
            xs = _mixer(xs, g_mix[i], bf(a_w_in[i]), a_conv_w[i], bf(a_w_out[i]), batch, seq)
        else:
            j = i - n_self
            first = i == n_self
            outs = _proj(xs, g_mix[i], bf(b_w_q[j]), q_scale, y=y,
                         g_kv=g_kv if first else None, w_kv=bf(w_kv) if first else None)
            outs = list(outs)
            if y is not None:
                xs = outs.pop(0)
                y = None
            q = outs.pop(0)
            if first:
                k, v = outs
            o = _attention(q, k, v, batch, seq)
            w_o = bf(b_w_o[j])
        assert y is None
        if i % 2 == 0:
            xs = _ffn(xs, g_ffn[i], ffn_w_gu, ffn_w_down, i // 2, o=o, w_o=w_o)
        else:
            m = i // 2
            xs, h_tm, idx, w = _router(xs, g_ffn[i], moe_w_router[m], o=o, w_o=w_o)
            y = _experts(h_tm, idx, w, moe_w_gu, moe_w_down, m, n)
    assert y is not None
    return _final_norm(xs, y, g_final).reshape(batch, seq, d)
```

```python
import functools
import math

import jax
import jax.numpy as jnp
from jax import lax
from jax.experimental import pallas as pl
from jax.experimental.pallas import tpu as pltpu

F32 = jnp.float32
BF16 = jnp.bfloat16

RMS_EPS = 1e-6
LOG2_E = math.log2(math.e)
N_HEADS = 16
N_EXPERTS = 8
TOP_K = 2

V7X_LANES = 128
V7X_SUBLANES = 8
V7X_VMEM_BYTES = 64 * 1024 * 1024
VMEM_LIMIT = V7X_VMEM_BYTES - 8 * 1024 * 1024

MIXER_ROWS = 512
FFN_ROWS = 512
PROJ_ROWS = 1024
ROUTER_ROWS = 512
EXPERT_ROWS = 512
ATTN_BLOCK = 256
ATTN_UNDERFLOW_LOG2 = -150.0
HEADS_PER_STEP = 2
CONV_CARRY_ROWS = V7X_SUBLANES
DMA_UNROLL = 64
EXPERT_FF_CHUNKS = 2


def _params(*semantics):
    return pltpu.CompilerParams(dimension_semantics=semantics,
                                vmem_limit_bytes=VMEM_LIMIT)


def _dot(a, b):
    return jnp.dot(a, b, preferred_element_type=F32)


def _rms_unit(x):
    return x * lax.rsqrt(jnp.mean(x * x, axis=-1, keepdims=True) + RMS_EPS)


def _load_token_major(ref, rows):
    parts = [ref[pl.ds(j, rows, stride=V7X_SUBLANES), :] for j in range(V7X_SUBLANES)]
    return jnp.concatenate(parts, axis=1)


def _store_token_major(ref, val, rows):
    for j in range(V7X_SUBLANES):
        ref[pl.ds(j, rows, stride=V7X_SUBLANES), :] = val[:, j * V7X_LANES:(j + 1) * V7X_LANES]


def _mixer_kernel(x_ref, g_ref, win_ref, cw_ref, wout_ref, out_ref, carry_ref, *, rows, d):
    s = pl.program_id(1)

    @pl.when(s == 0)
    def _():
        carry_ref[...] = jnp.zeros_like(carry_ref)

    x = x_ref[...]
    h = (_rms_unit(x) * g_ref[...]).astype(BF16)
    proj = _dot(h, win_ref[...])
    u = proj[:, :d] * proj[:, d:2 * d]
    b_gate = proj[:, 2 * d:]
    prev = carry_ref[...]
    row = lax.broadcasted_iota(jnp.int32, (rows, d), 0)
    u1 = jnp.where(row == 0, prev[CONV_CARRY_ROWS - 1:CONV_CARRY_ROWS, :], pltpu.roll(u, 1, 0))
    u2 = pltpu.roll(u, 2, 0)
    u2 = jnp.where(row == 0, prev[CONV_CARRY_ROWS - 2:CONV_CARRY_ROWS - 1, :], u2)
    u2 = jnp.where(row == 1, prev[CONV_CARRY_ROWS - 1:CONV_CARRY_ROWS, :], u2)
    carry_ref[...] = u[rows - CONV_CARRY_ROWS:, :]
    cw = cw_ref[...]
    conv = cw[0:1, :] * u2 + cw[1:2, :] * u1 + cw[2:3, :] * u
    mixed = (b_gate * conv).astype(BF16)
    out_ref[...] = x + _dot(mixed, wout_ref[...])


def _mixer(x, g, w_in, conv_w, w_out, batch, seq):
    n, d = x.shape
    rows = min(MIXER_ROWS, seq)
    tiles = seq // rows
    return pl.pallas_call(
        functools.partial(_mixer_kernel, rows=rows, d=d),
        grid=(batch, tiles),
        in_specs=[
            pl.BlockSpec((rows, d), lambda i, j: (i * tiles + j, 0)),
            pl.BlockSpec((1, d), lambda i, j: (0, 0)),
            pl.BlockSpec((d, 3 * d), lambda i, j: (0, 0)),
            pl.BlockSpec((3, d), lambda i, j: (0, 0)),
            pl.BlockSpec((d, d), lambda i, j: (0, 0)),
        ],
        out_specs=pl.BlockSpec((rows, d), lambda i, j: (i * tiles + j, 0)),
        out_shape=jax.ShapeDtypeStruct((n, d), F32),
        scratch_shapes=[pltpu.VMEM((CONV_CARRY_ROWS, d), F32)],
        compiler_params=_params("arbitrary", "arbitrary"),
        name="conv_mixer",
    )(x, g.reshape(1, d), w_in, conv_w, w_out)


def _ffn_kernel(*refs, has_o):
    if has_o:
        x_ref, o_ref, wo_ref, g_ref, wg_ref, wu_ref, wd_ref, out_ref, h_ref = refs
    else:
        x_ref, g_ref, wg_ref, wu_ref, wd_ref, out_ref, h_ref = refs

    @pl.when(pl.program_id(1) == 0)
    def _():
        x = x_ref[...]
        if has_o:
            x = x + _dot(o_ref[...], wo_ref[...])
        out_ref[...] = x
        h_ref[...] = (_rms_unit(x) * g_ref[...]).astype(BF16)

    h = h_ref[...]
    gate = _dot(h, wg_ref[...])
    up = _dot(h, wu_ref[...])
    act = (gate * jax.nn.sigmoid(gate) * up).astype(BF16)
    out_ref[...] += _dot(act, wd_ref[...])


def _ff_tile(d_ff):
    half = d_ff // 2
    return half if half % V7X_LANES == 0 else d_ff


def _lane_chunks(width, n_chunks):
    lane_groups = width // V7X_LANES
    bounds = [(lane_groups * c // n_chunks) * V7X_LANES for c in range(n_chunks + 1)]
    return [(a, b) for a, b in zip(bounds[:-1], bounds[1:]) if b > a]


def _ffn(x, g, w_gu, w_down, layer, o=None, w_o=None):
    n, d = x.shape
    d_ff = w_down.shape[1]
    rows = min(FFN_ROWS, n)
    tf = _ff_tile(d_ff)
    n_f = d_ff // tf
    has_o = o is not None
    row_spec = pl.BlockSpec((rows, d), lambda i, f: (i, 0))
    in_specs = [row_spec]
    args = [x]
    if has_o:
        in_specs += [row_spec, pl.BlockSpec((d, d), lambda i, f: (0, 0))]
        args += [o, w_o]
    in_specs += [
        pl.BlockSpec((1, d), lambda i, f: (0, 0)),
        pl.BlockSpec((None, d, tf), lambda i, f: (layer, 0, f)),
        pl.BlockSpec((None, d, tf), lambda i, f: (layer, 0, f + n_f)),
        pl.BlockSpec((None, tf, d), lambda i, f: (layer, f, 0)),
    ]
    args += [g.reshape(1, d), w_gu, w_gu, w_down]
    return pl.pallas_call(
        functools.partial(_ffn_kernel, has_o=has_o),
        grid=(n // rows, n_f),
        in_specs=in_specs,
        out_specs=row_spec,
        out_shape=jax.ShapeDtypeStruct((n, d), F32),
        scratch_shapes=[pltpu.VMEM((rows, d), BF16)],
        compiler_params=_params("arbitrary", "arbitrary"),
        name="dense_swiglu",
    )(*args)


def _proj_kernel(*refs, has_y, has_kv, rows, d, q_scale):
    refs = list(refs)
    x_ref = refs.pop(0)
    if has_y:
        y0_ref = refs.pop(0)
        y1_ref = refs.pop(0)
    gq_ref = refs.pop(0)
    wq_ref = refs.pop(0)
    if has_kv:
        gkv_ref = refs.pop(0)
        wkv_ref = refs.pop(0)
    if has_y:
        xs_ref = refs.pop(0)
    q_ref = refs.pop(0)
    if has_kv:
        k_ref = refs.pop(0)
        v_ref = refs.pop(0)

    x = x_ref[...]
    if has_y:
        x = x + _load_token_major(y0_ref, rows) + _load_token_major(y1_ref, rows)
        xs_ref[...] = x
    xn = _rms_unit(x)
    q = _dot((xn * gq_ref[...]).astype(BF16), wq_ref[...])
    q_ref[...] = (q * q_scale).astype(BF16)
    if has_kv:
        kv = _dot((xn * gkv_ref[...]).astype(BF16), wkv_ref[...])
        k_ref[...] = kv[:, :d].astype(BF16)
        v_ref[...] = kv[:, d:].astype(BF16)


def _proj(x, g_q, w_q, q_scale, y=None, g_kv=None, w_kv=None):
    n, d = x.shape
    rows = min(PROJ_ROWS, n)
    has_y = y is not None
    has_kv = w_kv is not None
    row_spec = pl.BlockSpec((rows, d), lambda i: (i, 0))
    vec_spec = pl.BlockSpec((1, d), lambda i: (0, 0))
    in_specs = [row_spec]
    args = [x]
    if has_y:
        blocks_per_slot = n // rows
        tm_spec0 = pl.BlockSpec((rows * V7X_SUBLANES, V7X_LANES), lambda i: (i, 0))
        tm_spec1 = pl.BlockSpec((rows * V7X_SUBLANES, V7X_LANES),
                                lambda i: (i + blocks_per_slot, 0))
        in_specs += [tm_spec0, tm_spec1]
        args += [y, y]
    in_specs += [vec_spec, pl.BlockSpec((d, d), lambda i: (0, 0))]
    args += [g_q.reshape(1, d), w_q]
    if has_kv:
        in_specs += [vec_spec, pl.BlockSpec((d, 2 * d), lambda i: (0, 0))]
        args += [g_kv.reshape(1, d), w_kv]
    out_specs = []
    out_shape = []
    if has_y:
        out_specs.append(row_spec)
        out_shape.append(jax.ShapeDtypeStruct((n, d), F32))
    n_bf16 = 3 if has_kv else 1
    out_specs += [row_spec] * n_bf16
    out_shape += [jax.ShapeDtypeStruct((n, d), BF16)] * n_bf16
    return pl.pallas_call(
        functools.partial(_proj_kernel, has_y=has_y, has_kv=has_kv, rows=rows, d=d,
                          q_scale=q_scale),
        grid=(n // rows,),
        in_specs=in_specs,
        out_specs=out_specs,
        out_shape=out_shape,
        compiler_params=_params("arbitrary"),
        name="norm_qkv_proj",
    )(*args)


def _attn_kernel(q_ref, k_ref, v_ref, later_ref, o_ref, qs_ref, acc_ref, run_ref, *,
                 blk, dh, q_blocks):
    lane = lax.broadcasted_iota(jnp.int32, (1, HEADS_PER_STEP * dh), 1)
    head_mask = [(lane >= hh * dh) & (lane < (hh + 1) * dh) for hh in range(HEADS_PER_STEP)]
    stacked = HEADS_PER_STEP * blk

    def visit(key_blocks, diagonal_first):
        later = later_ref[...]
        qs = qs_ref[...]
        run = run_ref[...]
        total = None
        for pos, kb in enumerate(key_blocks):
            masked = diagonal_first and pos == 0
            start = pl.multiple_of(kb * blk, blk)
            k_blk = k_ref[pl.ds(start, blk), :]
            v_blk = v_ref[pl.ds(start, blk), :]
            z2 = lax.dot_general(qs, k_blk, (((1,), (1,)), ((), ())), preferred_element_type=F32)
            soft = jnp.log2(1.0 + jnp.exp2(-jnp.abs(z2)))
            log_beta = jnp.minimum(z2, 0.0) - soft
            log_1m_beta = log_beta - z2
            if masked:
                t_idx = lax.broadcasted_iota(jnp.int32, (stacked, blk), 0) & (blk - 1)
                s_idx = lax.broadcasted_iota(jnp.int32, (stacked, blk), 1)
                causal = s_idx < t_idx
                log_1m_beta = jnp.where(causal, log_1m_beta, 0.0)
            suffix = _dot(log_1m_beta.astype(BF16), later)
            attn = jnp.exp2(log_beta + suffix + run)
            if masked:
                attn = jnp.where(causal, attn, 0.0)
            av = _dot(attn.astype(BF16), v_blk)
            out = jnp.where(head_mask[0], av[:blk], av[blk:])
            total = out if total is None else total + out
            run = run + jnp.sum(log_1m_beta, axis=-1, keepdims=True)
        acc_ref[...] += total
        run_ref[...] = run

    def exhausted():
        return jnp.max(run_ref[...]) <= ATTN_UNDERFLOW_LOG2

    def q_tile(qi, carry):
        q_rows = pl.ds(pl.multiple_of(qi * blk, blk), blk)
        q = q_ref[q_rows, :]
        for hh in range(HEADS_PER_STEP):
            qs_ref[hh * blk:(hh + 1) * blk, :] = jnp.where(head_mask[hh], q, jnp.zeros_like(q))
        acc_ref[...] = jnp.zeros_like(acc_ref)
        run_ref[...] = jnp.zeros_like(run_ref)

        @pl.when(qi == 0)
        def _():
            visit([qi], True)

        @pl.when(qi > 0)
        def _():
            visit([qi, qi - 1], True)

        def pair(it, done):
            kb = qi - 2 - 2 * it

            @pl.when(jnp.logical_not(done))
            def _():
                visit([kb, kb - 1], False)

            return done | exhausted()

        done = lax.fori_loop(0, jnp.maximum(qi - 1, 0) // 2, pair, exhausted())

        @pl.when(jnp.logical_not(done) & (qi >= 2) & (qi % 2 == 0))
        def _():
            visit([0], False)

        o_ref[q_rows, :] = acc_ref[...].astype(BF16)
        return carry

    lax.fori_loop(0, q_blocks, q_tile, 0)


def _attention(q, k, v, batch, seq):
    n, d = q.shape
    dh = d // N_HEADS
    blk = min(ATTN_BLOCK, seq)
    width = HEADS_PER_STEP * dh
    q_blocks = seq // blk
    idx = jnp.arange(blk, dtype=jnp.int32)
    later = (idx[:, None] > idx[None, :]).astype(BF16)
    col_blocks = d // width
    seq_spec = pl.BlockSpec((seq, width), lambda g: (g // col_blocks, g % col_blocks))
    return pl.pallas_call(
        functools.partial(_attn_kernel, blk=blk, dh=dh, q_blocks=q_blocks),
        grid=(batch * col_blocks,),
        in_specs=[seq_spec, seq_spec, seq_spec, pl.BlockSpec((blk, blk), lambda g: (0, 0))],
        out_specs=seq_spec,
        out_shape=jax.ShapeDtypeStruct((n, d), BF16),
        scratch_shapes=[pltpu.VMEM((HEADS_PER_STEP * blk, width), BF16),
                        pltpu.VMEM((blk, width), F32),
                        pltpu.VMEM((HEADS_PER_STEP * blk, 1), F32)],
        compiler_params=_params("arbitrary"),
        name="stick_breaking_attention",
    )(q, k, v, later)


def _router_kernel(*refs, has_o, rows):
    refs = list(refs)
    x_ref = refs.pop(0)
    if has_o:
        o_ref = refs.pop(0)
        wo_ref = refs.pop(0)
    g_ref = refs.pop(0)
    wr_hi_ref = refs.pop(0)
    wr_lo_ref = refs.pop(0)
    if has_o:
        xs_ref = refs.pop(0)
    h_ref, idx_ref, w_ref = refs

    x = x_ref[...]
    if has_o:
        x = x + _dot(o_ref[...], wo_ref[...])
        xs_ref[...] = x
    h = _rms_unit(x) * g_ref[...]
    _store_token_major(h_ref, h, rows)

    h_hi = h.astype(BF16)
    h_lo = (h - h_hi.astype(F32)).astype(BF16)
    logits = _dot(h_hi, wr_hi_ref[...]) + (_dot(h_lo, wr_hi_ref[...]) + _dot(h_hi, wr_lo_ref[...]))

    lane = lax.broadcasted_iota(jnp.int32, logits.shape, 1)
    lane_f = lane.astype(F32)
    neg_inf = jnp.float32(-jnp.inf)
    no_lane = jnp.float32(V7X_LANES)
    lg = jnp.where(lane < N_EXPERTS, logits, neg_inf)
    m1 = jnp.max(lg, axis=-1, keepdims=True)
    i1 = jnp.min(jnp.where(lg == m1, lane_f, no_lane), axis=-1, keepdims=True)
    lg2 = jnp.where(lane_f == i1, neg_inf, lg)
    m2 = jnp.max(lg2, axis=-1, keepdims=True)
    i2 = jnp.min(jnp.where(lg2 == m2, lane_f, no_lane), axis=-1, keepdims=True)
    e2 = jnp.exp(m2 - m1)
    w1 = 1.0 / (1.0 + e2)
    w2 = e2 * w1
    idx_ref[...] = jnp.where(lane == 0, i1, jnp.where(lane == 1, i2, 0.0)).astype(jnp.int32)
    w_ref[...] = jnp.where(lane == 0, w1, jnp.where(lane == 1, w2, 0.0))


def _router(x, g, w_router, o=None, w_o=None):
    n, d = x.shape
    rows = min(ROUTER_ROWS, n)
    has_o = o is not None
    wr = jnp.zeros((d, V7X_LANES), F32).at[:, :N_EXPERTS].set(w_router)
    wr_hi = wr.astype(BF16)
    wr_lo = (wr - wr_hi.astype(F32)).astype(BF16)
    row_spec = pl.BlockSpec((rows, d), lambda i: (i, 0))
    lane_spec = pl.BlockSpec((rows, V7X_LANES), lambda i: (i, 0))
    in_specs = [row_spec]
    args = [x]
    if has_o:
        in_specs += [row_spec, pl.BlockSpec((d, d), lambda i: (0, 0))]
        args += [o, w_o]
    in_specs += [pl.BlockSpec((1, d), lambda i: (0, 0)),
                 pl.BlockSpec((d, V7X_LANES), lambda i: (0, 0)),
                 pl.BlockSpec((d, V7X_LANES), lambda i: (0, 0))]
    args += [g.reshape(1, d), wr_hi, wr_lo]
    out_specs = []
    out_shape = []
    if has_o:
        out_specs.append(row_spec)
        out_shape.append(jax.ShapeDtypeStruct((n, d), F32))
    out_specs += [pl.BlockSpec((rows * V7X_SUBLANES, V7X_LANES), lambda i: (i, 0)),
                  lane_spec, lane_spec]
    out_shape += [jax.ShapeDtypeStruct((n * V7X_SUBLANES, V7X_LANES), F32),
                  jax.ShapeDtypeStruct((n, V7X_LANES), jnp.int32),
                  jax.ShapeDtypeStruct((n, V7X_LANES), F32)]
    outs = pl.pallas_call(
        functools.partial(_router_kernel, has_o=has_o, rows=rows),
        grid=(n // rows,),
        in_specs=in_specs,
        out_specs=out_specs,
        out_shape=out_shape,
        compiler_params=_params("arbitrary"),
        name="router_top2",
    )(*args)
    if has_o:
        xs, h_tm, idx, w = outs
    else:
        h_tm, idx, w = outs
        xs = x
    return xs, h_tm, idx[:, :TOP_K], w[:, :TOP_K]


def _plan_routing(idx, w, rows):
    n = idx.shape[0]
    n_pairs = n * TOP_K
    n_tiles = n_pairs // rows + N_EXPERTS
    e_flat = idx.reshape(n_pairs)
    pair_id = jnp.arange(n_pairs, dtype=jnp.int32)
    _, order, w_sorted = lax.sort((e_flat, pair_id, w.reshape(n_pairs)), num_keys=1, is_stable=True)
    experts = jnp.arange(N_EXPERTS, dtype=jnp.int32)
    counts = jnp.sum((e_flat[:, None] == experts[None, :]).astype(jnp.int32), axis=0)
    starts = jnp.cumsum(counts) - counts
    pad_counts = ((counts + rows - 1) // rows) * rows
    pad_ends = jnp.cumsum(pad_counts)
    pad_starts = pad_ends - pad_counts
    tile_start = jnp.arange(n_tiles, dtype=jnp.int32) * rows
    e_of_tile = jnp.sum((tile_start[:, None] >= pad_ends[None, :]).astype(jnp.int32), axis=1)
    tile_expert = jnp.minimum(e_of_tile, N_EXPERTS - 1)
    tile_within = tile_start - pad_starts[tile_expert]
    tile_count = jnp.where(e_of_tile < N_EXPERTS,
                           jnp.clip(counts[tile_expert] - tile_within, 0, rows), 0)
    lane = jnp.arange(rows, dtype=jnp.int32)
    sorted_pos = jnp.clip((starts[tile_expert] + tile_within)[:, None] + lane[None, :],
                          0, n_pairs - 1)
    valid = lane[None, :] < tile_count[:, None]
    pair = order[sorted_pos]
    token = pair // TOP_K
    slot = pair % TOP_K
    src = jnp.where(valid, token, 0)
    dst = jnp.where(valid, slot * n + token, 0)
    gate = jnp.where(valid, w_sorted[sorted_pos], 0.0)
    return (src.reshape(n_tiles, 1, rows), dst.reshape(n_tiles, 1, rows),
            gate.reshape(n_tiles * rows, 1), tile_expert, tile_count)


def _expert_kernel(te_ref, tc_ref, src_ref, src_next_ref, dst_ref, gate_ref, h_hbm,
                   wg_ref, wu_ref, wd_ref, y_hbm,
                   gbuf_ref, ybuf_ref, hbf_ref, acc_ref, gsem, ssem, *, rows, n_f, n_tiles):
    i = pl.program_id(0)
    j = pl.program_id(1)
    count = tc_ref[i]
    valid = count > 0
    full = count == rows
    first_step = j == 0
    last_step = j == n_f - 1
    has_next = (i + 1 < n_tiles) & (tc_ref[jnp.minimum(i + 1, n_tiles - 1)] > 0)
    prev_in_flight = (i > 0) & (tc_ref[jnp.maximum(i - 1, 0)] == rows)
    tile_rows = rows * V7X_SUBLANES

    def row_window(r):
        return pl.ds(pl.multiple_of(r * V7X_SUBLANES, V7X_SUBLANES), V7X_SUBLANES)

    def start_gather(idx_ref, r, slot):
        pltpu.make_async_copy(h_hbm.at[row_window(idx_ref[0, r]), :],
                              gbuf_ref.at[slot, row_window(r), :], gsem.at[slot]).start()

    def wait_gather(slot):
        pltpu.make_async_copy(h_hbm.at[pl.ds(0, tile_rows), :], gbuf_ref.at[slot],
                              gsem.at[slot]).wait()

    def scatter_copy(r):
        return pltpu.make_async_copy(ybuf_ref.at[row_window(r), :],
                                     y_hbm.at[row_window(dst_ref[0, r]), :], ssem.at[0])

    def wait_full_scatter():
        pltpu.make_async_copy(ybuf_ref, y_hbm.at[pl.ds(0, tile_rows), :], ssem.at[0]).wait()

    def for_all_rows(fn):
        def body(b, carry):
            for u in range(DMA_UNROLL):
                fn(b * DMA_UNROLL + u)
            return carry
        lax.fori_loop(0, rows // DMA_UNROLL, body, 0)

    def for_rows(n_rows, fn):
        def body(r, carry):
            fn(r)
            return carry
        lax.fori_loop(0, n_rows, body, 0)

    @pl.when(first_step & (i == 0))
    def _():
        for_all_rows(lambda r: start_gather(src_ref, r, 0))

    for slot in range(2):
        mine = valid & first_step & (i % 2 == slot)

        @pl.when(mine)
        def _(slot=slot):
            wait_gather(slot)
            hbf_ref[...] = _load_token_major(gbuf_ref.at[slot], rows).astype(BF16)
            acc_ref[...] = jnp.zeros_like(acc_ref)

        @pl.when(mine & has_next)
        def _(slot=slot):
            for_all_rows(lambda r: start_gather(src_next_ref, r, 1 - slot))

    @pl.when(valid)
    def _():
        h = hbf_ref[...]
        part = None
        for c0, c1 in _lane_chunks(wg_ref.shape[1], EXPERT_FF_CHUNKS):
            gate = _dot(h, wg_ref[:, c0:c1].astype(BF16))
            up = _dot(h, wu_ref[:, c0:c1].astype(BF16))
            act = (gate * jax.nn.sigmoid(gate) * up).astype(BF16)
            contrib = _dot(act, wd_ref[c0:c1, :].astype(BF16))
            part = contrib if part is None else part + contrib
        acc_ref[...] += part

    @pl.when(valid & last_step & prev_in_flight)
    def _():
        wait_full_scatter()

    @pl.when(valid & last_step)
    def _():
        _store_token_major(ybuf_ref, acc_ref[...] * gate_ref[...], rows)

    @pl.when(valid & last_step & full)
    def _():
        for_all_rows(lambda r: scatter_copy(r).start())

    @pl.when(valid & last_step & full & jnp.logical_not(has_next))
    def _():
        wait_full_scatter()

    @pl.when(valid & last_step & jnp.logical_not(full))
    def _():
        for_rows(count, lambda r: scatter_copy(r).start())
        for_rows(count, lambda r: scatter_copy(r).wait())


def _experts(h_tm, idx, w, w_gu, w_down, layer, n):
    d = w_down.shape[3]
    d_ff = w_down.shape[2]
    rows = min(EXPERT_ROWS, n)
    assert rows % DMA_UNROLL == 0
    tf = _ff_tile(d_ff)
    n_f = d_ff // tf
    src, dst, gate, tile_expert, tile_count = _plan_routing(idx, w, rows)
    n_tiles = src.shape[0]

    def f_idx(i, j, tc):
        return jnp.where(tc[i] > 0, jnp.where(i % 2 == 0, j, n_f - 1 - j), 0)

    idx_spec = functools.partial(pl.BlockSpec, (None, 1, rows), memory_space=pltpu.SMEM)
    grid_spec = pltpu.PrefetchScalarGridSpec(
        num_scalar_prefetch=2,
        grid=(n_tiles, n_f),
        in_specs=[
            idx_spec(lambda i, j, te, tc: (i, 0, 0)),
            idx_spec(lambda i, j, te, tc: (jnp.minimum(i + 1, n_tiles - 1), 0, 0)),
            idx_spec(lambda i, j, te, tc: (i, 0, 0)),
            pl.BlockSpec((rows, 1), lambda i, j, te, tc: (i, 0)),
            pl.BlockSpec(memory_space=pl.ANY),
            pl.BlockSpec((None, None, d, tf),
                         lambda i, j, te, tc: (layer, te[i], 0, f_idx(i, j, tc))),
            pl.BlockSpec((None, None, d, tf),
                         lambda i, j, te, tc: (layer, te[i], 0, f_idx(i, j, tc) + n_f)),
            pl.BlockSpec((None, None, tf, d),
                         lambda i, j, te, tc: (layer, te[i], f_idx(i, j, tc), 0)),
        ],
        out_specs=pl.BlockSpec(memory_space=pl.ANY),
        scratch_shapes=[
            pltpu.VMEM((2, rows * V7X_SUBLANES, V7X_LANES), F32),
            pltpu.VMEM((rows * V7X_SUBLANES, V7X_LANES), F32),
            pltpu.VMEM((rows, d), BF16),
            pltpu.VMEM((rows, d), F32),
            pltpu.SemaphoreType.DMA((2,)),
            pltpu.SemaphoreType.DMA((1,)),
        ],
    )
    return pl.pallas_call(
        functools.partial(_expert_kernel, rows=rows, n_f=n_f, n_tiles=n_tiles),
        grid_spec=grid_spec,
        out_shape=jax.ShapeDtypeStruct((n * TOP_K * V7X_SUBLANES, V7X_LANES), F32),
        compiler_params=_params("arbitrary", "arbitrary"),
        name="routed_swiglu",
    )(tile_expert, tile_count, src, src, dst, gate, h_tm, w_gu, w_gu, w_down)


def _final_kernel(x_ref, y0_ref, y1_ref, g_ref, out_ref, *, rows):
    x = x_ref[...] + _load_token_major(y0_ref, rows) + _load_token_major(y1_ref, rows)
    out_ref[...] = _rms_unit(x) * g_ref[...]


def _final_norm(x, y, g):
    n, d = x.shape
    rows = min(PROJ_ROWS, n)
    blocks_per_slot = n // rows
    row_spec = pl.BlockSpec((rows, d), lambda i: (i, 0))
    return pl.pallas_call(
        functools.partial(_final_kernel, rows=rows),
        grid=(n // rows,),
        in_specs=[
            row_spec,
            pl.BlockSpec((rows * V7X_SUBLANES, V7X_LANES), lambda i: (i, 0)),
            pl.BlockSpec((rows * V7X_SUBLANES, V7X_LANES), lambda i: (i + blocks_per_slot, 0)),
            pl.BlockSpec((1, d), lambda i: (0, 0)),
        ],
        out_specs=row_spec,
        out_shape=jax.ShapeDtypeStruct((n, d), F32),
        compiler_params=_params("arbitrary"),
        name="final_norm",
    )(x, y, y, g.reshape(1, d))


def kernel(x, g_mix, g_ffn, g_final, a_w_in, a_conv_w, a_w_out, g_kv, w_kv, b_w_q, b_w_o,
           ffn_w_gu, ffn_w_down, moe_w_router, moe_w_gu, moe_w_down):
    batch, seq, d = x.shape
    n = batch * seq
    depth = g_mix.shape[0]
    n_self = a_w_in.shape[0]
    q_scale = LOG2_E / math.sqrt(d // N_HEADS)

    def bf(a):
        return a.astype(BF16)

    ffn_w_gu, ffn_w_down = bf(ffn_w_gu), bf(ffn_w_down)

    xs = x.reshape(n, d)
    y = None
    k = v = None
    for i in range(depth):
        o = w_o = None
        if i < n_self:
            assert y is None
            xs = _mixer(xs, g_mix[i], bf(a_w_in[i]), a_conv_w[i], bf(a_w_out[i]), batch, seq)
        else:
            j = i - n_self
            first = i == n_self
            outs = _proj(xs, g_mix[i], bf(b_w_q[j]), q_scale, y=y,
                         g_kv=g_kv if first else None, w_kv=bf(w_kv) if first else None)
            outs = list(outs)
            if y is not None:
                xs = outs.pop(0)
                y = None
            q = outs.pop(0)
            if first:
                k, v = outs
            o = _attention(q, k, v, batch, seq)
            w_o = bf(b_w_o[j])
        assert y is None
        if i % 2 == 0:
            xs = _ffn(xs, g_ffn[i], ffn_w_gu, ffn_w_down, i // 2, o=o, w_o=w_o)
        else:
            m = i // 2
            xs, h_tm, idx, w = _router(xs, g_ffn[i], moe_w_router[m], o=o, w_o=w_o)
            y = _experts(h_tm, idx, w, moe_w_gu, moe_w_down, m, n)
    assert y is not None
    return _final_norm(xs, y, g_final).reshape(batch, seq, d)
```

```python
import functools
import math

import jax
import jax.numpy as jnp
from jax import lax
from jax.experimental import pallas as pl
from jax.experimental.pallas import tpu as pltpu

F32 = jnp.float32
BF16 = jnp.bfloat16

RMS_EPS = 1e-6
LOG2_E = math.log2(math.e)
N_HEADS = 16
N_EXPERTS = 8
TOP_K = 2

V7X_LANES = 128
V7X_SUBLANES = 8
V7X_MXU_WIDTH = 256
V7X_VMEM_BYTES = 64 * 1024 * 1024
VMEM_LIMIT = V7X_VMEM_BYTES - 8 * 1024 * 1024

MIXER_ROWS = 512
FFN_ROWS = 512
PROJ_ROWS = 1024
ROUTER_ROWS = 512
EXPERT_ROWS = 512
ATTN_BLOCK = 256
ATTN_UNDERFLOW_LOG2 = -150.0
HEADS_PER_STEP = 2
CONV_CARRY_ROWS = V7X_SUBLANES
DMA_UNROLL = 64
EXPERT_FF_TILES = 3


def _params(*semantics):
    return pltpu.CompilerParams(dimension_semantics=semantics,
                                vmem_limit_bytes=VMEM_LIMIT)


def _dot(a, b):
    return jnp.dot(a, b, preferred_element_type=F32)


def _rms_unit(x):
    return x * lax.rsqrt(jnp.mean(x * x, axis=-1, keepdims=True) + RMS_EPS)


def _load_token_major(ref, rows):
    parts = [ref[pl.ds(j, rows, stride=V7X_SUBLANES), :] for j in range(V7X_SUBLANES)]
    return jnp.concatenate(parts, axis=1)


def _store_token_major(ref, val, rows):
    for j in range(V7X_SUBLANES):
        ref[pl.ds(j, rows, stride=V7X_SUBLANES), :] = val[:, j * V7X_LANES:(j + 1) * V7X_LANES]


def _mixer_kernel(x_ref, g_ref, win_ref, cw_ref, wout_ref, out_ref, carry_ref, *, rows, d):
    s = pl.program_id(1)

    @pl.when(s == 0)
    def _():
        carry_ref[...] = jnp.zeros_like(carry_ref)

    x = x_ref[...]
    h = (_rms_unit(x) * g_ref[...]).astype(BF16)
    proj = _dot(h, win_ref[...])
    u = proj[:, :d] * proj[:, d:2 * d]
    b_gate = proj[:, 2 * d:]
    prev = carry_ref[...]
    row = lax.broadcasted_iota(jnp.int32, (rows, d), 0)
    u1 = jnp.where(row == 0, prev[CONV_CARRY_ROWS - 1:CONV_CARRY_ROWS, :], pltpu.roll(u, 1, 0))
    u2 = pltpu.roll(u, 2, 0)
    u2 = jnp.where(row == 0, prev[CONV_CARRY_ROWS - 2:CONV_CARRY_ROWS - 1, :], u2)
    u2 = jnp.where(row == 1, prev[CONV_CARRY_ROWS - 1:CONV_CARRY_ROWS, :], u2)
    carry_ref[...] = u[rows - CONV_CARRY_ROWS:, :]
    cw = cw_ref[...]
    conv = cw[0:1, :] * u2 + cw[1:2, :] * u1 + cw[2:3, :] * u
    mixed = (b_gate * conv).astype(BF16)
    out_ref[...] = x + _dot(mixed, wout_ref[...])


def _mixer(x, g, w_in, conv_w, w_out, batch, seq):
    n, d = x.shape
    rows = min(MIXER_ROWS, seq)
    tiles = seq // rows
    return pl.pallas_call(
        functools.partial(_mixer_kernel, rows=rows, d=d),
        grid=(batch, tiles),
        in_specs=[
            pl.BlockSpec((rows, d), lambda i, j: (i * tiles + j, 0)),
            pl.BlockSpec((1, d), lambda i, j: (0, 0)),
            pl.BlockSpec((d, 3 * d), lambda i, j: (0, 0)),
            pl.BlockSpec((3, d), lambda i, j: (0, 0)),
            pl.BlockSpec((d, d), lambda i, j: (0, 0)),
        ],
        out_specs=pl.BlockSpec((rows, d), lambda i, j: (i * tiles + j, 0)),
        out_shape=jax.ShapeDtypeStruct((n, d), F32),
        scratch_shapes=[pltpu.VMEM((CONV_CARRY_ROWS, d), F32)],
        compiler_params=_params("arbitrary", "arbitrary"),
        name="conv_mixer",
    )(x, g.reshape(1, d), w_in, conv_w, w_out)


def _ffn_kernel(*refs, has_o, d_ff):
    if has_o:
        x_ref, o_ref, wo_ref, g_ref, wgu_ref, wd_ref, out_ref = refs
    else:
        x_ref, g_ref, wgu_ref, wd_ref, out_ref = refs
    x = x_ref[...]
    if has_o:
        x = x + _dot(o_ref[...], wo_ref[...])
    h = (_rms_unit(x) * g_ref[...]).astype(BF16)
    acc = x
    for c0 in range(0, d_ff, V7X_MXU_WIDTH):
        c1 = c0 + V7X_MXU_WIDTH
        gate = _dot(h, wgu_ref[:, c0:c1])
        up = _dot(h, wgu_ref[:, d_ff + c0:d_ff + c1])
        act = (gate * jax.nn.sigmoid(gate) * up).astype(BF16)
        acc = acc + _dot(act, wd_ref[c0:c1, :])
    out_ref[...] = acc


def _ffn(x, g, w_gu, w_down, layer, o=None, w_o=None):
    n, d = x.shape
    d_ff = w_down.shape[1]
    assert d_ff % V7X_MXU_WIDTH == 0
    rows = min(FFN_ROWS, n)
    has_o = o is not None
    row_spec = pl.BlockSpec((rows, d), lambda i: (i, 0))
    once = pl.Buffered(1)
    in_specs = [row_spec]
    args = [x]
    if has_o:
        in_specs += [row_spec, pl.BlockSpec((d, d), lambda i: (0, 0), pipeline_mode=once)]
        args += [o, w_o]
    in_specs += [
        pl.BlockSpec((1, d), lambda i: (0, 0)),
        pl.BlockSpec((None, d, 2 * d_ff), lambda i: (layer, 0, 0), pipeline_mode=once),
        pl.BlockSpec((None, d_ff, d), lambda i: (layer, 0, 0), pipeline_mode=once),
    ]
    args += [g.reshape(1, d), w_gu, w_down]
    return pl.pallas_call(
        functools.partial(_ffn_kernel, has_o=has_o, d_ff=d_ff),
        grid=(n // rows,),
        in_specs=in_specs,
        out_specs=row_spec,
        out_shape=jax.ShapeDtypeStruct((n, d), F32),
        compiler_params=_params("arbitrary"),
        name="dense_swiglu",
    )(*args)


def _proj_kernel(*refs, has_y, has_kv, rows, d, q_scale):
    refs = list(refs)
    x_ref = refs.pop(0)
    if has_y:
        y0_ref = refs.pop(0)
        y1_ref = refs.pop(0)
    gq_ref = refs.pop(0)
    wq_ref = refs.pop(0)
    if has_kv:
        gkv_ref = refs.pop(0)
        wkv_ref = refs.pop(0)
    if has_y:
        xs_ref = refs.pop(0)
    q_ref = refs.pop(0)
    if has_kv:
        k_ref = refs.pop(0)
        v_ref = refs.pop(0)

    x = x_ref[...]
    if has_y:
        x = x + _load_token_major(y0_ref, rows) + _load_token_major(y1_ref, rows)
        xs_ref[...] = x
    xn = _rms_unit(x)
    q = _dot((xn * gq_ref[...]).astype(BF16), wq_ref[...])
    q_ref[...] = (q * q_scale).astype(BF16)
    if has_kv:
        kv = _dot((xn * gkv_ref[...]).astype(BF16), wkv_ref[...])
        k_ref[...] = kv[:, :d].astype(BF16)
        v_ref[...] = kv[:, d:].astype(BF16)


def _proj(x, g_q, w_q, q_scale, y=None, g_kv=None, w_kv=None):
    n, d = x.shape
    rows = min(PROJ_ROWS, n)
    has_y = y is not None
    has_kv = w_kv is not None
    row_spec = pl.BlockSpec((rows, d), lambda i: (i, 0))
    vec_spec = pl.BlockSpec((1, d), lambda i: (0, 0))
    in_specs = [row_spec]
    args = [x]
    if has_y:
        blocks_per_slot = n // rows
        tm_spec0 = pl.BlockSpec((rows * V7X_SUBLANES, V7X_LANES), lambda i: (i, 0))
        tm_spec1 = pl.BlockSpec((rows * V7X_SUBLANES, V7X_LANES),
                                lambda i: (i + blocks_per_slot, 0))
        in_specs += [tm_spec0, tm_spec1]
        args += [y, y]
    in_specs += [vec_spec, pl.BlockSpec((d, d), lambda i: (0, 0))]
    args += [g_q.reshape(1, d), w_q]
    if has_kv:
        in_specs += [vec_spec, pl.BlockSpec((d, 2 * d), lambda i: (0, 0))]
        args += [g_kv.reshape(1, d), w_kv]
    out_specs = []
    out_shape = []
    if has_y:
        out_specs.append(row_spec)
        out_shape.append(jax.ShapeDtypeStruct((n, d), F32))
    n_bf16 = 3 if has_kv else 1
    out_specs += [row_spec] * n_bf16
    out_shape += [jax.ShapeDtypeStruct((n, d), BF16)] * n_bf16
    return pl.pallas_call(
        functools.partial(_proj_kernel, has_y=has_y, has_kv=has_kv, rows=rows, d=d,
                          q_scale=q_scale),
        grid=(n // rows,),
        in_specs=in_specs,
        out_specs=out_specs,
        out_shape=out_shape,
        compiler_params=_params("arbitrary"),
        name="norm_qkv_proj",
    )(*args)


def _attn_kernel(q_ref, k_ref, v_ref, later_ref, o_ref, qs_ref, acc_ref, run_ref, *,
                 blk, dh, q_blocks):
    lane = lax.broadcasted_iota(jnp.int32, (1, HEADS_PER_STEP * dh), 1)
    head_mask = [(lane >= hh * dh) & (lane < (hh + 1) * dh) for hh in range(HEADS_PER_STEP)]
    stacked = HEADS_PER_STEP * blk

    def visit(key_blocks, diagonal_first):
        later = later_ref[...]
        qs = qs_ref[...]
        run = run_ref[...]
        total = None
        for pos, kb in enumerate(key_blocks):
            masked = diagonal_first and pos == 0
            start = pl.multiple_of(kb * blk, blk)
            k_blk = k_ref[pl.ds(start, blk), :]
            v_blk = v_ref[pl.ds(start, blk), :]
            z2 = lax.dot_general(qs, k_blk, (((1,), (1,)), ((), ())), preferred_element_type=F32)
            soft = jnp.log2(1.0 + jnp.exp2(-jnp.abs(z2)))
            log_beta = jnp.minimum(z2, 0.0) - soft
            log_1m_beta = log_beta - z2
            if masked:
                t_idx = lax.broadcasted_iota(jnp.int32, (stacked, blk), 0) & (blk - 1)
                s_idx = lax.broadcasted_iota(jnp.int32, (stacked, blk), 1)
                causal = s_idx < t_idx
                log_1m_beta = jnp.where(causal, log_1m_beta, 0.0)
            suffix = _dot(log_1m_beta.astype(BF16), later)
            attn = jnp.exp2(log_beta + suffix + run)
            if masked:
                attn = jnp.where(causal, attn, 0.0)
            av = _dot(attn.astype(BF16), v_blk)
            out = jnp.where(head_mask[0], av[:blk], av[blk:])
            total = out if total is None else total + out
            run = run + jnp.sum(log_1m_beta, axis=-1, keepdims=True)
        acc_ref[...] += total
        run_ref[...] = run

    def exhausted():
        return jnp.max(run_ref[...]) <= ATTN_UNDERFLOW_LOG2

    def q_tile(qi, carry):
        q_rows = pl.ds(pl.multiple_of(qi * blk, blk), blk)
        q = q_ref[q_rows, :]
        for hh in range(HEADS_PER_STEP):
            qs_ref[hh * blk:(hh + 1) * blk, :] = jnp.where(head_mask[hh], q, jnp.zeros_like(q))
        acc_ref[...] = jnp.zeros_like(acc_ref)
        run_ref[...] = jnp.zeros_like(run_ref)

        @pl.when(qi == 0)
        def _():
            visit([qi], True)

        @pl.when(qi > 0)
        def _():
            visit([qi, qi - 1], True)

        def pair(it, done):
            kb = qi - 2 - 2 * it

            @pl.when(jnp.logical_not(done))
            def _():
                visit([kb, kb - 1], False)

            return done | exhausted()

        done = lax.fori_loop(0, jnp.maximum(qi - 1, 0) // 2, pair, exhausted())

        @pl.when(jnp.logical_not(done) & (qi >= 2) & (qi % 2 == 0))
        def _():
            visit([0], False)

        o_ref[q_rows, :] = acc_ref[...].astype(BF16)
        return carry

    lax.fori_loop(0, q_blocks, q_tile, 0)


def _attention(q, k, v, batch, seq):
    n, d = q.shape
    dh = d // N_HEADS
    blk = min(ATTN_BLOCK, seq)
    width = HEADS_PER_STEP * dh
    q_blocks = seq // blk
    idx = jnp.arange(blk, dtype=jnp.int32)
    later = (idx[:, None] > idx[None, :]).astype(BF16)
    col_blocks = d // width
    seq_spec = pl.BlockSpec((seq, width), lambda g: (g // col_blocks, g % col_blocks))
    return pl.pallas_call(
        functools.partial(_attn_kernel, blk=blk, dh=dh, q_blocks=q_blocks),
        grid=(batch * col_blocks,),
        in_specs=[seq_spec, seq_spec, seq_spec, pl.BlockSpec((blk, blk), lambda g: (0, 0))],
        out_specs=seq_spec,
        out_shape=jax.ShapeDtypeStruct((n, d), BF16),
        scratch_shapes=[pltpu.VMEM((HEADS_PER_STEP * blk, width), BF16),
                        pltpu.VMEM((blk, width), F32),
                        pltpu.VMEM((HEADS_PER_STEP * blk, 1), F32)],
        compiler_params=_params("arbitrary"),
        name="stick_breaking_attention",
    )(q, k, v, later)


def _router_kernel(*refs, has_o, rows):
    refs = list(refs)
    x_ref = refs.pop(0)
    if has_o:
        o_ref = refs.pop(0)
        wo_ref = refs.pop(0)
    g_ref = refs.pop(0)
    wr_hi_ref = refs.pop(0)
    wr_lo_ref = refs.pop(0)
    if has_o:
        xs_ref = refs.pop(0)
    h_ref, idx_ref, w_ref = refs

    x = x_ref[...]
    if has_o:
        x = x + _dot(o_ref[...], wo_ref[...])
        xs_ref[...] = x
    h = _rms_unit(x) * g_ref[...]
    _store_token_major(h_ref, h, rows)

    h_hi = h.astype(BF16)
    h_lo = (h - h_hi.astype(F32)).astype(BF16)
    logits = _dot(h_hi, wr_hi_ref[...]) + (_dot(h_lo, wr_hi_ref[...]) + _dot(h_hi, wr_lo_ref[...]))

    lane = lax.broadcasted_iota(jnp.int32, logits.shape, 1)
    lane_f = lane.astype(F32)
    neg_inf = jnp.float32(-jnp.inf)
    no_lane = jnp.float32(V7X_LANES)
    lg = jnp.where(lane < N_EXPERTS, logits, neg_inf)
    m1 = jnp.max(lg, axis=-1, keepdims=True)
    i1 = jnp.min(jnp.where(lg == m1, lane_f, no_lane), axis=-1, keepdims=True)
    lg2 = jnp.where(lane_f == i1, neg_inf, lg)
    m2 = jnp.max(lg2, axis=-1, keepdims=True)
    i2 = jnp.min(jnp.where(lg2 == m2, lane_f, no_lane), axis=-1, keepdims=True)
    e2 = jnp.exp(m2 - m1)
    w1 = 1.0 / (1.0 + e2)
    w2 = e2 * w1
    idx_ref[...] = jnp.where(lane == 0, i1, jnp.where(lane == 1, i2, 0.0)).astype(jnp.int32)
    w_ref[...] = jnp.where(lane == 0, w1, jnp.where(lane == 1, w2, 0.0))


def _router(x, g, w_router, o=None, w_o=None):
    n, d = x.shape
    rows = min(ROUTER_ROWS, n)
    has_o = o is not None
    wr = jnp.zeros((d, V7X_LANES), F32).at[:, :N_EXPERTS].set(w_router)
    wr_hi = wr.astype(BF16)
    wr_lo = (wr - wr_hi.astype(F32)).astype(BF16)
    row_spec = pl.BlockSpec((rows, d), lambda i: (i, 0))
    lane_spec = pl.BlockSpec((rows, V7X_LANES), lambda i: (i, 0))
    in_specs = [row_spec]
    args = [x]
    if has_o:
        in_specs += [row_spec, pl.BlockSpec((d, d), lambda i: (0, 0))]
        args += [o, w_o]
    in_specs += [pl.BlockSpec((1, d), lambda i: (0, 0)),
                 pl.BlockSpec((d, V7X_LANES), lambda i: (0, 0)),
                 pl.BlockSpec((d, V7X_LANES), lambda i: (0, 0))]
    args += [g.reshape(1, d), wr_hi, wr_lo]
    out_specs = []
    out_shape = []
    if has_o:
        out_specs.append(row_spec)
        out_shape.append(jax.ShapeDtypeStruct((n, d), F32))
    out_specs += [pl.BlockSpec((rows * V7X_SUBLANES, V7X_LANES), lambda i: (i, 0)),
                  lane_spec, lane_spec]
    out_shape += [jax.ShapeDtypeStruct((n * V7X_SUBLANES, V7X_LANES), F32),
                  jax.ShapeDtypeStruct((n, V7X_LANES), jnp.int32),
                  jax.ShapeDtypeStruct((n, V7X_LANES), F32)]
    outs = pl.pallas_call(
        functools.partial(_router_kernel, has_o=has_o, rows=rows),
        grid=(n // rows,),
        in_specs=in_specs,
        out_specs=out_specs,
        out_shape=out_shape,
        compiler_params=_params("arbitrary"),
        name="router_top2",
    )(*args)
    if has_o:
        xs, h_tm, idx, w = outs
    else:
        h_tm, idx, w = outs
        xs = x
    return xs, h_tm, idx[:, :TOP_K], w[:, :TOP_K]


def _plan_routing(idx, w, rows):
    n = idx.shape[0]
    n_pairs = n * TOP_K
    n_tiles = n_pairs // rows + N_EXPERTS
    e_flat = idx.reshape(n_pairs)
    pair_id = jnp.arange(n_pairs, dtype=jnp.int32)
    _, order, w_sorted = lax.sort((e_flat, pair_id, w.reshape(n_pairs)), num_keys=1, is_stable=True)
    experts = jnp.arange(N_EXPERTS, dtype=jnp.int32)
    counts = jnp.sum((e_flat[:, None] == experts[None, :]).astype(jnp.int32), axis=0)
    starts = jnp.cumsum(counts) - counts
    pad_counts = ((counts + rows - 1) // rows) * rows
    pad_ends = jnp.cumsum(pad_counts)
    pad_starts = pad_ends - pad_counts
    tile_start = jnp.arange(n_tiles, dtype=jnp.int32) * rows
    e_of_tile = jnp.sum((tile_start[:, None] >= pad_ends[None, :]).astype(jnp.int32), axis=1)
    tile_expert = jnp.minimum(e_of_tile, N_EXPERTS - 1)
    tile_within = tile_start - pad_starts[tile_expert]
    tile_count = jnp.where(e_of_tile < N_EXPERTS,
                           jnp.clip(counts[tile_expert] - tile_within, 0, rows), 0)
    lane = jnp.arange(rows, dtype=jnp.int32)
    sorted_pos = jnp.clip((starts[tile_expert] + tile_within)[:, None] + lane[None, :],
                          0, n_pairs - 1)
    valid = lane[None, :] < tile_count[:, None]
    pair = order[sorted_pos]
    token = pair // TOP_K
    slot = pair % TOP_K
    src = jnp.where(valid, token, 0)
    dst = jnp.where(valid, slot * n + token, 0)
    gate = jnp.where(valid, w_sorted[sorted_pos], 0.0)
    return (src.reshape(n_tiles, 1, rows), dst.reshape(n_tiles, 1, rows),
            gate.reshape(n_tiles * rows, 1), tile_expert, tile_count)


def _expert_kernel(te_ref, tc_ref, src_ref, src_next_ref, dst_ref, gate_ref, h_hbm,
                   wg_ref, wu_ref, wd_ref, y_hbm,
                   gbuf_ref, ybuf_ref, hbf_ref, acc_ref, gsem, ssem, *,
                   rows, n_f, n_tiles, overlap):
    i = pl.program_id(0)
    j = pl.program_id(1)
    count = tc_ref[i]
    valid = count > 0
    full = count == rows
    first_step = j == 0
    last_step = j == n_f - 1
    has_next = (i + 1 < n_tiles) & (tc_ref[jnp.minimum(i + 1, n_tiles - 1)] > 0)
    prev_in_flight = (i > 0) & (tc_ref[jnp.maximum(i - 1, 0)] == rows)
    tile_rows = rows * V7X_SUBLANES

    def row_window(r):
        return pl.ds(pl.multiple_of(r * V7X_SUBLANES, V7X_SUBLANES), V7X_SUBLANES)

    def start_gather(idx_ref, r, slot):
        pltpu.make_async_copy(h_hbm.at[row_window(idx_ref[0, r]), :],
                              gbuf_ref.at[slot, row_window(r), :], gsem.at[slot]).start()

    def wait_gather(slot):
        pltpu.make_async_copy(h_hbm.at[pl.ds(0, tile_rows), :], gbuf_ref.at[slot],
                              gsem.at[slot]).wait()

    def scatter_copy(r):
        return pltpu.make_async_copy(ybuf_ref.at[row_window(r), :],
                                     y_hbm.at[row_window(dst_ref[0, r]), :], ssem.at[0])

    def wait_full_scatter():
        pltpu.make_async_copy(ybuf_ref, y_hbm.at[pl.ds(0, tile_rows), :], ssem.at[0]).wait()

    def for_all_rows(fn):
        def body(b, carry):
            for u in range(DMA_UNROLL):
                fn(b * DMA_UNROLL + u)
            return carry
        lax.fori_loop(0, rows // DMA_UNROLL, body, 0)

    def for_rows(n_rows, fn):
        def body(r, carry):
            fn(r)
            return carry
        lax.fori_loop(0, n_rows, body, 0)

    @pl.when(first_step & (i == 0))
    def _():
        for_all_rows(lambda r: start_gather(src_ref, r, 0))

    for slot in range(2):
        mine = valid & first_step & (i % 2 == slot)

        @pl.when(mine)
        def _(slot=slot):
            wait_gather(slot)
            hbf_ref[...] = _load_token_major(gbuf_ref.at[slot], rows).astype(BF16)
            acc_ref[...] = jnp.zeros_like(acc_ref)

        @pl.when(mine & has_next)
        def _(slot=slot):
            for_all_rows(lambda r: start_gather(src_next_ref, r, 1 - slot))

    def matmul_step(skip):
        h = hbf_ref[...]
        part = None
        width = wg_ref.shape[3]
        step = V7X_MXU_WIDTH * EXPERT_FF_TILES
        for c0 in range(skip, width, step):
            c1 = min(c0 + step, width)
            gate = _dot(h, wg_ref[0, 0, :, c0:c1].astype(BF16))
            up = _dot(h, wu_ref[0, 0, :, c0:c1].astype(BF16))
            act = (gate * jax.nn.sigmoid(gate) * up).astype(BF16)
            contrib = _dot(act, wd_ref[0, 0, c0:c1, :].astype(BF16))
            part = contrib if part is None else part + contrib
        acc_ref[...] += part

    upper = jnp.where(i % 2 == 0, j, n_f - 1 - j) == 1

    @pl.when(valid & jnp.logical_not(upper))
    def _():
        matmul_step(0)

    @pl.when(valid & upper)
    def _():
        matmul_step(overlap)

    @pl.when(valid & last_step & prev_in_flight)
    def _():
        wait_full_scatter()

    @pl.when(valid & last_step)
    def _():
        _store_token_major(ybuf_ref, acc_ref[...] * gate_ref[...], rows)

    @pl.when(valid & last_step & full)
    def _():
        for_all_rows(lambda r: scatter_copy(r).start())

    @pl.when(valid & last_step & full & jnp.logical_not(has_next))
    def _():
        wait_full_scatter()

    @pl.when(valid & last_step & jnp.logical_not(full))
    def _():
        for_rows(count, lambda r: scatter_copy(r).start())
        for_rows(count, lambda r: scatter_copy(r).wait())


def _experts(h_tm, idx, w, w_gu, w_down, layer, n):
    d = w_down.shape[3]
    d_ff = w_down.shape[2]
    rows = min(EXPERT_ROWS, n)
    assert rows % DMA_UNROLL == 0
    n_f = 2
    assert d_ff % V7X_MXU_WIDTH == 0 and d_ff // V7X_MXU_WIDTH >= n_f
    tf = -(-(d_ff // V7X_MXU_WIDTH) // n_f) * V7X_MXU_WIDTH
    overlap = n_f * tf - d_ff
    src, dst, gate, tile_expert, tile_count = _plan_routing(idx, w, rows)
    n_tiles = src.shape[0]

    def f_off(i, j, tc):
        pos = jnp.where(tc[i] > 0, jnp.where(i % 2 == 0, j, n_f - 1 - j), 0)
        return pl.multiple_of(pos * (d_ff - tf), V7X_MXU_WIDTH)

    def window(rows_cols, offsets):
        return pl.BlockSpec((pl.Element(1), pl.Element(1)) + tuple(pl.Element(s) for s in rows_cols),
                            offsets)

    idx_spec = functools.partial(pl.BlockSpec, (None, 1, rows), memory_space=pltpu.SMEM)
    grid_spec = pltpu.PrefetchScalarGridSpec(
        num_scalar_prefetch=2,
        grid=(n_tiles, n_f),
        in_specs=[
            idx_spec(lambda i, j, te, tc: (i, 0, 0)),
            idx_spec(lambda i, j, te, tc: (jnp.minimum(i + 1, n_tiles - 1), 0, 0)),
            idx_spec(lambda i, j, te, tc: (i, 0, 0)),
            pl.BlockSpec((rows, 1), lambda i, j, te, tc: (i, 0)),
            pl.BlockSpec(memory_space=pl.ANY),
            window((d, tf), lambda i, j, te, tc: (layer, te[i], 0, f_off(i, j, tc))),
            window((d, tf), lambda i, j, te, tc: (
                layer, te[i], 0, pl.multiple_of(d_ff + f_off(i, j, tc), V7X_MXU_WIDTH))),
            window((tf, d), lambda i, j, te, tc: (layer, te[i], f_off(i, j, tc), 0)),
        ],
        out_specs=pl.BlockSpec(memory_space=pl.ANY),
        scratch_shapes=[
            pltpu.VMEM((2, rows * V7X_SUBLANES, V7X_LANES), F32),
            pltpu.VMEM((rows * V7X_SUBLANES, V7X_LANES), F32),
            pltpu.VMEM((rows, d), BF16),
            pltpu.VMEM((rows, d), F32),
            pltpu.SemaphoreType.DMA((2,)),
            pltpu.SemaphoreType.DMA((1,)),
        ],
    )
    return pl.pallas_call(
        functools.partial(_expert_kernel, rows=rows, n_f=n_f, n_tiles=n_tiles, overlap=overlap),
        grid_spec=grid_spec,
        out_shape=jax.ShapeDtypeStruct((n * TOP_K * V7X_SUBLANES, V7X_LANES), F32),
        compiler_params=_params("arbitrary", "arbitrary"),
        name="routed_swiglu",
    )(tile_expert, tile_count, src, src, dst, gate, h_tm, w_gu, w_gu, w_down)


def _final_kernel(x_ref, y0_ref, y1_ref, g_ref, out_ref, *, rows):
    x = x_ref[...] + _load_token_major(y0_ref, rows) + _load_token_major(y1_ref, rows)
    out_ref[...] = _rms_unit(x) * g_ref[...]


def _final_norm(x, y, g):
    n, d = x.shape
    rows = min(PROJ_ROWS, n)
    blocks_per_slot = n // rows
    row_spec = pl.BlockSpec((rows, d), lambda i: (i, 0))
    return pl.pallas_call(
        functools.partial(_final_kernel, rows=rows),
        grid=(n // rows,),
        in_specs=[
            row_spec,
            pl.BlockSpec((rows * V7X_SUBLANES, V7X_LANES), lambda i: (i, 0)),
            pl.BlockSpec((rows * V7X_SUBLANES, V7X_LANES), lambda i: (i + blocks_per_slot, 0)),
            pl.BlockSpec((1, d), lambda i: (0, 0)),
        ],
        out_specs=row_spec,
        out_shape=jax.ShapeDtypeStruct((n, d), F32),
        compiler_params=_params("arbitrary"),
        name="final_norm",
    )(x, y, y, g.reshape(1, d))


def kernel(x, g_mix, g_ffn, g_final, a_w_in, a_conv_w, a_w_out, g_kv, w_kv, b_w_q, b_w_o,
           ffn_w_gu, ffn_w_down, moe_w_router, moe_w_gu, moe_w_down):
    batch, seq, d = x.shape
    n = batch * seq
    depth = g_mix.shape[0]
    n_self = a_w_in.shape[0]
    q_scale = LOG2_E / math.sqrt(d // N_HEADS)

    def bf(a):
        return a.astype(BF16)

    ffn_w_gu, ffn_w_down = bf(ffn_w_gu), bf(ffn_w_down)

    xs = x.reshape(n, d)
    y = None
    k = v = None
    for i in range(depth):
        o = w_o = None
        if i < n_self:
            assert y is None
            xs = _mixer(xs, g_mix[i], bf(a_w_in[i]), a_conv_w[i], bf(a_w_out[i]), batch, seq)
        else:
            j = i - n_self
            first = i == n_self
            outs = _proj(xs, g_mix[i], bf(b_w_q[j]), q_scale, y=y,
                         g_kv=g_kv if first else None, w_kv=bf(w_kv) if first else None)
            outs = list(outs)
            if y is not None:
                xs = outs.pop(0)
                y = None
            q = outs.pop(0)
            if first:
                k, v = outs
            o = _attention(q, k, v, batch, seq)
            w_o = bf(b_w_o[j])
        assert y is None
        if i % 2 == 0:
            xs = _ffn(xs, g_ffn[i], ffn_w_gu, ffn_w_down, i // 2, o=o, w_o=w_o)
        else:
            m = i // 2
            xs, h_tm, idx, w = _router(xs, g_ffn[i], moe_w_router[m], o=o, w_o=w_o)
            y = _experts(h_tm, idx, w, moe_w_gu, moe_w_down, m, n)
    assert y is not None
    return _final_norm(xs, y, g_final).reshape(batch, seq, d)
```

```python
import functools
import math

import jax
import jax.numpy as jnp
from jax import lax
from jax.experimental import pallas as pl
from jax.experimental.pallas import tpu as pltpu

F32 = jnp.float32
BF16 = jnp.bfloat16

RMS_EPS = 1e-6
LOG2_E = math.log2(math.e)
N_HEADS = 16
N_EXPERTS = 8
TOP_K = 2

V7X_LANES = 128
V7X_SUBLANES = 8
V7X_MXU_WIDTH = 256
V7X_VMEM_BYTES = 64 * 1024 * 1024
VMEM_LIMIT = V7X_VMEM_BYTES - 8 * 1024 * 1024
EXPERT_VMEM_LIMIT = V7X_VMEM_BYTES - 2 * 1024 * 1024

MIXER_ROWS = 512
FFN_ROWS = 512
PROJ_ROWS = 1024
ROUTER_ROWS = 1024
EXPERT_ROWS = 1024
ATTN_BLOCK = 256
ATTN_UNDERFLOW_LOG2 = -150.0
HEADS_PER_STEP = 2
CONV_CARRY_ROWS = V7X_SUBLANES
DMA_UNROLL = 64
EXPERT_FF_TILES = 1


def _params(*semantics, vmem_limit=VMEM_LIMIT):
    return pltpu.CompilerParams(dimension_semantics=semantics, vmem_limit_bytes=vmem_limit)


def _dot(a, b):
    return jnp.dot(a, b, preferred_element_type=F32)


def _rms_unit(x):
    return x * lax.rsqrt(jnp.mean(x * x, axis=-1, keepdims=True) + RMS_EPS)


def _load_token_major(ref, rows):
    parts = [ref[pl.ds(j, rows, stride=V7X_SUBLANES), :] for j in range(V7X_SUBLANES)]
    return jnp.concatenate(parts, axis=1)


def _store_token_major(ref, val, rows):
    for j in range(V7X_SUBLANES):
        ref[pl.ds(j, rows, stride=V7X_SUBLANES), :] = val[:, j * V7X_LANES:(j + 1) * V7X_LANES]


def _mixer_kernel(x_ref, g_ref, win_ref, cw_ref, wout_ref, out_ref, carry_ref, *, rows, d):
    s = pl.program_id(1)

    @pl.when(s == 0)
    def _():
        carry_ref[...] = jnp.zeros_like(carry_ref)

    x = x_ref[...]
    h = (_rms_unit(x) * g_ref[...]).astype(BF16)
    proj = _dot(h, win_ref[...])
    u = proj[:, :d] * proj[:, d:2 * d]
    b_gate = proj[:, 2 * d:]
    prev = carry_ref[...]
    row = lax.broadcasted_iota(jnp.int32, (rows, d), 0)
    u1 = jnp.where(row == 0, prev[CONV_CARRY_ROWS - 1:CONV_CARRY_ROWS, :], pltpu.roll(u, 1, 0))
    u2 = pltpu.roll(u, 2, 0)
    u2 = jnp.where(row == 0, prev[CONV_CARRY_ROWS - 2:CONV_CARRY_ROWS - 1, :], u2)
    u2 = jnp.where(row == 1, prev[CONV_CARRY_ROWS - 1:CONV_CARRY_ROWS, :], u2)
    carry_ref[...] = u[rows - CONV_CARRY_ROWS:, :]
    cw = cw_ref[...]
    conv = cw[0:1, :] * u2 + cw[1:2, :] * u1 + cw[2:3, :] * u
    mixed = (b_gate * conv).astype(BF16)
    out_ref[...] = x + _dot(mixed, wout_ref[...])


def _mixer(x, g, w_in, conv_w, w_out, batch, seq):
    n, d = x.shape
    rows = min(MIXER_ROWS, seq)
    tiles = seq // rows
    return pl.pallas_call(
        functools.partial(_mixer_kernel, rows=rows, d=d),
        grid=(batch, tiles),
        in_specs=[
            pl.BlockSpec((rows, d), lambda i, j: (i * tiles + j, 0)),
            pl.BlockSpec((1, d), lambda i, j: (0, 0)),
            pl.BlockSpec((d, 3 * d), lambda i, j: (0, 0)),
            pl.BlockSpec((3, d), lambda i, j: (0, 0)),
            pl.BlockSpec((d, d), lambda i, j: (0, 0)),
        ],
        out_specs=pl.BlockSpec((rows, d), lambda i, j: (i * tiles + j, 0)),
        out_shape=jax.ShapeDtypeStruct((n, d), F32),
        scratch_shapes=[pltpu.VMEM((CONV_CARRY_ROWS, d), F32)],
        compiler_params=_params("arbitrary", "arbitrary"),
        name="conv_mixer",
    )(x, g.reshape(1, d), w_in, conv_w, w_out)


def _ffn_kernel(*refs, has_o, d_ff):
    if has_o:
        x_ref, o_ref, wo_ref, g_ref, wgu_ref, wd_ref, out_ref = refs
    else:
        x_ref, g_ref, wgu_ref, wd_ref, out_ref = refs
    x = x_ref[...]
    if has_o:
        x = x + _dot(o_ref[...], wo_ref[...])
    h = (_rms_unit(x) * g_ref[...]).astype(BF16)
    acc = x
    for c0 in range(0, d_ff, V7X_MXU_WIDTH):
        c1 = c0 + V7X_MXU_WIDTH
        gate = _dot(h, wgu_ref[:, c0:c1])
        up = _dot(h, wgu_ref[:, d_ff + c0:d_ff + c1])
        act = (gate * jax.nn.sigmoid(gate) * up).astype(BF16)
        acc = acc + _dot(act, wd_ref[c0:c1, :])
    out_ref[...] = acc


def _ffn(x, g, w_gu, w_down, layer, o=None, w_o=None):
    n, d = x.shape
    d_ff = w_down.shape[1]
    assert d_ff % V7X_MXU_WIDTH == 0
    rows = min(FFN_ROWS, n)
    has_o = o is not None
    row_spec = pl.BlockSpec((rows, d), lambda i: (i, 0))
    once = pl.Buffered(1)
    in_specs = [row_spec]
    args = [x]
    if has_o:
        in_specs += [row_spec, pl.BlockSpec((d, d), lambda i: (0, 0), pipeline_mode=once)]
        args += [o, w_o]
    in_specs += [
        pl.BlockSpec((1, d), lambda i: (0, 0)),
        pl.BlockSpec((None, d, 2 * d_ff), lambda i: (layer, 0, 0), pipeline_mode=once),
        pl.BlockSpec((None, d_ff, d), lambda i: (layer, 0, 0), pipeline_mode=once),
    ]
    args += [g.reshape(1, d), w_gu, w_down]
    return pl.pallas_call(
        functools.partial(_ffn_kernel, has_o=has_o, d_ff=d_ff),
        grid=(n // rows,),
        in_specs=in_specs,
        out_specs=row_spec,
        out_shape=jax.ShapeDtypeStruct((n, d), F32),
        compiler_params=_params("arbitrary"),
        name="dense_swiglu",
    )(*args)


def _proj_kernel(*refs, has_y, has_kv, rows, d, q_scale):
    refs = list(refs)
    x_ref = refs.pop(0)
    if has_y:
        y0_ref = refs.pop(0)
        y1_ref = refs.pop(0)
    gq_ref = refs.pop(0)
    wq_ref = refs.pop(0)
    if has_kv:
        gkv_ref = refs.pop(0)
        wkv_ref = refs.pop(0)
    if has_y:
        xs_ref = refs.pop(0)
    q_ref = refs.pop(0)
    if has_kv:
        k_ref = refs.pop(0)
        v_ref = refs.pop(0)

    x = x_ref[...]
    if has_y:
        x = x + _load_token_major(y0_ref, rows) + _load_token_major(y1_ref, rows)
        xs_ref[...] = x
    xn = _rms_unit(x)
    q = _dot((xn * gq_ref[...]).astype(BF16), wq_ref[...])
    q_ref[...] = (q * q_scale).astype(BF16)
    if has_kv:
        kv = _dot((xn * gkv_ref[...]).astype(BF16), wkv_ref[...])
        k_ref[...] = kv[:, :d].astype(BF16)
        v_ref[...] = kv[:, d:].astype(BF16)


def _proj(x, g_q, w_q, q_scale, y=None, g_kv=None, w_kv=None):
    n, d = x.shape
    rows = min(PROJ_ROWS, n)
    has_y = y is not None
    has_kv = w_kv is not None
    row_spec = pl.BlockSpec((rows, d), lambda i: (i, 0))
    vec_spec = pl.BlockSpec((1, d), lambda i: (0, 0))
    in_specs = [row_spec]
    args = [x]
    if has_y:
        blocks_per_slot = n // rows
        tm_spec0 = pl.BlockSpec((rows * V7X_SUBLANES, V7X_LANES), lambda i: (i, 0))
        tm_spec1 = pl.BlockSpec((rows * V7X_SUBLANES, V7X_LANES),
                                lambda i: (i + blocks_per_slot, 0))
        in_specs += [tm_spec0, tm_spec1]
        args += [y, y]
    in_specs += [vec_spec, pl.BlockSpec((d, d), lambda i: (0, 0))]
    args += [g_q.reshape(1, d), w_q]
    if has_kv:
        in_specs += [vec_spec, pl.BlockSpec((d, 2 * d), lambda i: (0, 0))]
        args += [g_kv.reshape(1, d), w_kv]
    out_specs = []
    out_shape = []
    if has_y:
        out_specs.append(row_spec)
        out_shape.append(jax.ShapeDtypeStruct((n, d), F32))
    n_bf16 = 3 if has_kv else 1
    out_specs += [row_spec] * n_bf16
    out_shape += [jax.ShapeDtypeStruct((n, d), BF16)] * n_bf16
    return pl.pallas_call(
        functools.partial(_proj_kernel, has_y=has_y, has_kv=has_kv, rows=rows, d=d,
                          q_scale=q_scale),
        grid=(n // rows,),
        in_specs=in_specs,
        out_specs=out_specs,
        out_shape=out_shape,
        compiler_params=_params("arbitrary"),
        name="norm_qkv_proj",
    )(*args)


def _attn_kernel(q_ref, k_ref, v_ref, later_ref, o_ref, qs_ref, acc_ref, run_ref, *,
                 blk, dh, q_blocks):
    lane = lax.broadcasted_iota(jnp.int32, (1, HEADS_PER_STEP * dh), 1)
    head_mask = [(lane >= hh * dh) & (lane < (hh + 1) * dh) for hh in range(HEADS_PER_STEP)]
    stacked = HEADS_PER_STEP * blk

    def visit(key_blocks, diagonal_first):
        later = later_ref[...]
        qs = qs_ref[...]
        run = run_ref[...]
        total = None
        for pos, kb in enumerate(key_blocks):
            masked = diagonal_first and pos == 0
            start = pl.multiple_of(kb * blk, blk)
            k_blk = k_ref[pl.ds(start, blk), :]
            v_blk = v_ref[pl.ds(start, blk), :]
            z2 = lax.dot_general(qs, k_blk, (((1,), (1,)), ((), ())), preferred_element_type=F32)
            soft = jnp.log2(1.0 + jnp.exp2(-jnp.abs(z2)))
            log_beta = jnp.minimum(z2, 0.0) - soft
            log_1m_beta = log_beta - z2
            if masked:
                t_idx = lax.broadcasted_iota(jnp.int32, (stacked, blk), 0) & (blk - 1)
                s_idx = lax.broadcasted_iota(jnp.int32, (stacked, blk), 1)
                causal = s_idx < t_idx
                log_1m_beta = jnp.where(causal, log_1m_beta, 0.0)
            suffix = _dot(log_1m_beta.astype(BF16), later)
            attn = jnp.exp2(log_beta + suffix + run)
            if masked:
                attn = jnp.where(causal, attn, 0.0)
            av = _dot(attn.astype(BF16), v_blk)
            out = jnp.where(head_mask[0], av[:blk], av[blk:])
            total = out if total is None else total + out
            run = run + jnp.sum(log_1m_beta, axis=-1, keepdims=True)
        acc_ref[...] += total
        run_ref[...] = run

    def exhausted():
        return jnp.max(run_ref[...]) <= ATTN_UNDERFLOW_LOG2

    def q_tile(qi, carry):
        q_rows = pl.ds(pl.multiple_of(qi * blk, blk), blk)
        q = q_ref[q_rows, :]
        for hh in range(HEADS_PER_STEP):
            qs_ref[hh * blk:(hh + 1) * blk, :] = jnp.where(head_mask[hh], q, jnp.zeros_like(q))
        acc_ref[...] = jnp.zeros_like(acc_ref)
        run_ref[...] = jnp.zeros_like(run_ref)

        @pl.when(qi == 0)
        def _():
            visit([qi], True)

        @pl.when(qi > 0)
        def _():
            visit([qi, qi - 1], True)

        def pair(it, done):
            kb = qi - 2 - 2 * it

            @pl.when(jnp.logical_not(done))
            def _():
                visit([kb, kb - 1], False)

            return done | exhausted()

        done = lax.fori_loop(0, jnp.maximum(qi - 1, 0) // 2, pair, exhausted())

        @pl.when(jnp.logical_not(done) & (qi >= 2) & (qi % 2 == 0))
        def _():
            visit([0], False)

        o_ref[q_rows, :] = acc_ref[...].astype(BF16)
        return carry

    lax.fori_loop(0, q_blocks, q_tile, 0)


def _attention(q, k, v, batch, seq):
    n, d = q.shape
    dh = d // N_HEADS
    blk = min(ATTN_BLOCK, seq)
    width = HEADS_PER_STEP * dh
    q_blocks = seq // blk
    idx = jnp.arange(blk, dtype=jnp.int32)
    later = (idx[:, None] > idx[None, :]).astype(BF16)
    col_blocks = d // width
    seq_spec = pl.BlockSpec((seq, width), lambda g: (g // col_blocks, g % col_blocks))
    return pl.pallas_call(
        functools.partial(_attn_kernel, blk=blk, dh=dh, q_blocks=q_blocks),
        grid=(batch * col_blocks,),
        in_specs=[seq_spec, seq_spec, seq_spec, pl.BlockSpec((blk, blk), lambda g: (0, 0))],
        out_specs=seq_spec,
        out_shape=jax.ShapeDtypeStruct((n, d), BF16),
        scratch_shapes=[pltpu.VMEM((HEADS_PER_STEP * blk, width), BF16),
                        pltpu.VMEM((blk, width), F32),
                        pltpu.VMEM((HEADS_PER_STEP * blk, 1), F32)],
        compiler_params=_params("arbitrary"),
        name="stick_breaking_attention",
    )(q, k, v, later)


def _router_kernel(*refs, has_o, rows):
    refs = list(refs)
    x_ref = refs.pop(0)
    if has_o:
        o_ref = refs.pop(0)
        wo_ref = refs.pop(0)
    g_ref = refs.pop(0)
    wr_ref = refs.pop(0)
    if has_o:
        xs_ref = refs.pop(0)
    h_ref, idx_ref, w_ref = refs

    x = x_ref[...]
    if has_o:
        x = x + _dot(o_ref[...], wo_ref[...])
        xs_ref[...] = x
    h = _rms_unit(x) * g_ref[...]
    _store_token_major(h_ref, h, rows)

    h_hi = h.astype(BF16)
    h_lo = (h - h_hi.astype(F32)).astype(BF16)
    both = _dot(h_hi, wr_ref[...])
    logits = both[:, :V7X_LANES] + (both[:, V7X_LANES:] + _dot(h_lo, wr_ref[:, :V7X_LANES]))

    lane = lax.broadcasted_iota(jnp.int32, logits.shape, 1)
    lane_f = lane.astype(F32)
    neg_inf = jnp.float32(-jnp.inf)
    no_lane = jnp.float32(V7X_LANES)
    lg = jnp.where(lane < N_EXPERTS, logits, neg_inf)
    m1 = jnp.max(lg, axis=-1, keepdims=True)
    i1 = jnp.min(jnp.where(lg == m1, lane_f, no_lane), axis=-1, keepdims=True)
    lg2 = jnp.where(lane_f == i1, neg_inf, lg)
    m2 = jnp.max(lg2, axis=-1, keepdims=True)
    i2 = jnp.min(jnp.where(lg2 == m2, lane_f, no_lane), axis=-1, keepdims=True)
    e2 = jnp.exp(m2 - m1)
    w1 = 1.0 / (1.0 + e2)
    w2 = e2 * w1
    idx_ref[...] = jnp.where(lane == 0, i1, jnp.where(lane == 1, i2, 0.0)).astype(jnp.int32)
    w_ref[...] = jnp.where(lane == 0, w1, jnp.where(lane == 1, w2, 0.0))


def _router(x, g, w_router, o=None, w_o=None):
    n, d = x.shape
    rows = min(ROUTER_ROWS, n)
    has_o = o is not None
    wr = jnp.zeros((d, V7X_LANES), F32).at[:, :N_EXPERTS].set(w_router)
    wr_hi = wr.astype(BF16)
    wr_lo = (wr - wr_hi.astype(F32)).astype(BF16)
    wr_both = jnp.concatenate([wr_hi, wr_lo], axis=1)
    row_spec = pl.BlockSpec((rows, d), lambda i: (i, 0))
    lane_spec = pl.BlockSpec((rows, V7X_LANES), lambda i: (i, 0))
    in_specs = [row_spec]
    args = [x]
    if has_o:
        in_specs += [row_spec, pl.BlockSpec((d, d), lambda i: (0, 0))]
        args += [o, w_o]
    in_specs += [pl.BlockSpec((1, d), lambda i: (0, 0)),
                 pl.BlockSpec((d, 2 * V7X_LANES), lambda i: (0, 0))]
    args += [g.reshape(1, d), wr_both]
    out_specs = []
    out_shape = []
    if has_o:
        out_specs.append(row_spec)
        out_shape.append(jax.ShapeDtypeStruct((n, d), F32))
    out_specs += [pl.BlockSpec((rows * V7X_SUBLANES, V7X_LANES), lambda i: (i, 0)),
                  lane_spec, lane_spec]
    out_shape += [jax.ShapeDtypeStruct((n * V7X_SUBLANES, V7X_LANES), F32),
                  jax.ShapeDtypeStruct((n, V7X_LANES), jnp.int32),
                  jax.ShapeDtypeStruct((n, V7X_LANES), F32)]
    outs = pl.pallas_call(
        functools.partial(_router_kernel, has_o=has_o, rows=rows),
        grid=(n // rows,),
        in_specs=in_specs,
        out_specs=out_specs,
        out_shape=out_shape,
        compiler_params=_params("arbitrary"),
        name="router_top2",
    )(*args)
    if has_o:
        xs, h_tm, idx, w = outs
    else:
        h_tm, idx, w = outs
        xs = x
    return xs, h_tm, idx[:, :TOP_K], w[:, :TOP_K]


def _plan_routing(idx, w, rows):
    n = idx.shape[0]
    n_pairs = n * TOP_K
    n_tiles = n_pairs // rows + N_EXPERTS
    e_flat = idx.reshape(n_pairs)
    pair_id = jnp.arange(n_pairs, dtype=jnp.int32)
    _, order, w_sorted = lax.sort((e_flat, pair_id, w.reshape(n_pairs)), num_keys=1, is_stable=True)
    experts = jnp.arange(N_EXPERTS, dtype=jnp.int32)
    counts = jnp.sum((e_flat[:, None] == experts[None, :]).astype(jnp.int32), axis=0)
    starts = jnp.cumsum(counts) - counts
    pad_counts = ((counts + rows - 1) // rows) * rows
    pad_ends = jnp.cumsum(pad_counts)
    pad_starts = pad_ends - pad_counts
    tile_start = jnp.arange(n_tiles, dtype=jnp.int32) * rows
    e_of_tile = jnp.sum((tile_start[:, None] >= pad_ends[None, :]).astype(jnp.int32), axis=1)
    tile_expert = jnp.minimum(e_of_tile, N_EXPERTS - 1)
    tile_within = tile_start - pad_starts[tile_expert]
    tile_count = jnp.where(e_of_tile < N_EXPERTS,
                           jnp.clip(counts[tile_expert] - tile_within, 0, rows), 0)
    lane = jnp.arange(rows, dtype=jnp.int32)
    sorted_pos = jnp.clip((starts[tile_expert] + tile_within)[:, None] + lane[None, :],
                          0, n_pairs - 1)
    valid = lane[None, :] < tile_count[:, None]
    pair = order[sorted_pos]
    token = pair // TOP_K
    slot = pair % TOP_K
    src = jnp.where(valid, token, 0)
    dst = jnp.where(valid, slot * n + token, 0)
    gate = jnp.where(valid, w_sorted[sorted_pos], 0.0)
    return (src.reshape(n_tiles, 1, rows), dst.reshape(n_tiles, 1, rows),
            gate.reshape(n_tiles * rows, 1), tile_expert, tile_count)


def _expert_kernel(te_ref, tc_ref, src_ref, src_next_ref, dst_ref, gate_ref, h_hbm,
                   wg_ref, wu_ref, wd_ref, y_hbm,
                   gbuf_ref, ybuf_ref, hbf_ref, acc_ref, gsem, ssem, *,
                   rows, n_f, n_tiles, overlap):
    i = pl.program_id(0)
    j = pl.program_id(1)
    count = tc_ref[i]
    valid = count > 0
    full = count == rows
    first_step = j == 0
    last_step = j == n_f - 1
    has_next = (i + 1 < n_tiles) & (tc_ref[jnp.minimum(i + 1, n_tiles - 1)] > 0)
    prev_in_flight = (i > 0) & (tc_ref[jnp.maximum(i - 1, 0)] == rows)
    tile_rows = rows * V7X_SUBLANES

    def row_window(r):
        return pl.ds(pl.multiple_of(r * V7X_SUBLANES, V7X_SUBLANES), V7X_SUBLANES)

    def start_gather(idx_ref, r, slot):
        pltpu.make_async_copy(h_hbm.at[row_window(idx_ref[0, r]), :],
                              gbuf_ref.at[slot, row_window(r), :], gsem.at[slot]).start()

    def wait_gather(slot):
        pltpu.make_async_copy(h_hbm.at[pl.ds(0, tile_rows), :], gbuf_ref.at[slot],
                              gsem.at[slot]).wait()

    def scatter_copy(r):
        return pltpu.make_async_copy(ybuf_ref.at[row_window(r), :],
                                     y_hbm.at[row_window(dst_ref[0, r]), :], ssem.at[0])

    def wait_full_scatter():
        pltpu.make_async_copy(ybuf_ref, y_hbm.at[pl.ds(0, tile_rows), :], ssem.at[0]).wait()

    def for_all_rows(fn):
        def body(b, carry):
            for u in range(DMA_UNROLL):
                fn(b * DMA_UNROLL + u)
            return carry
        lax.fori_loop(0, rows // DMA_UNROLL, body, 0)

    def for_rows(n_rows, fn):
        def body(r, carry):
            fn(r)
            return carry
        lax.fori_loop(0, n_rows, body, 0)

    @pl.when(first_step & (i == 0))
    def _():
        for_all_rows(lambda r: start_gather(src_ref, r, 0))

    for slot in range(2):
        mine = valid & first_step & (i % 2 == slot)

        @pl.when(mine)
        def _(slot=slot):
            wait_gather(slot)
            hbf_ref[...] = _load_token_major(gbuf_ref.at[slot], rows).astype(BF16)
            acc_ref[...] = jnp.zeros_like(acc_ref)

        @pl.when(mine & has_next)
        def _(slot=slot):
            for_all_rows(lambda r: start_gather(src_next_ref, r, 1 - slot))

    def matmul_step(skip):
        h = hbf_ref[...]
        part = None
        width = wg_ref.shape[3]
        step = V7X_MXU_WIDTH * EXPERT_FF_TILES
        for c0 in range(skip, width, step):
            c1 = min(c0 + step, width)
            gate = _dot(h, wg_ref[0, 0, :, c0:c1].astype(BF16))
            up = _dot(h, wu_ref[0, 0, :, c0:c1].astype(BF16))
            act = (gate * jax.nn.sigmoid(gate) * up).astype(BF16)
            contrib = _dot(act, wd_ref[0, 0, c0:c1, :].astype(BF16))
            part = contrib if part is None else part + contrib
        acc_ref[...] += part

    upper = jnp.where(i % 2 == 0, j, n_f - 1 - j) == 1

    @pl.when(valid & jnp.logical_not(upper))
    def _():
        matmul_step(0)

    @pl.when(valid & upper)
    def _():
        matmul_step(overlap)

    @pl.when(valid & last_step & prev_in_flight)
    def _():
        wait_full_scatter()

    @pl.when(valid & last_step)
    def _():
        _store_token_major(ybuf_ref, acc_ref[...] * gate_ref[...], rows)

    @pl.when(valid & last_step & full)
    def _():
        for_all_rows(lambda r: scatter_copy(r).start())

    @pl.when(valid & last_step & full & jnp.logical_not(has_next))
    def _():
        wait_full_scatter()

    @pl.when(valid & last_step & jnp.logical_not(full))
    def _():
        for_rows(count, lambda r: scatter_copy(r).start())
        for_rows(count, lambda r: scatter_copy(r).wait())


def _experts(h_tm, idx, w, w_gu, w_down, layer, n):
    d = w_down.shape[3]
    d_ff = w_down.shape[2]
    rows = min(EXPERT_ROWS, n)
    assert rows % DMA_UNROLL == 0
    n_f = 2
    assert d_ff % V7X_MXU_WIDTH == 0 and d_ff // V7X_MXU_WIDTH >= n_f
    tf = -(-(d_ff // V7X_MXU_WIDTH) // n_f) * V7X_MXU_WIDTH
    overlap = n_f * tf - d_ff
    src, dst, gate, tile_expert, tile_count = _plan_routing(idx, w, rows)
    n_tiles = src.shape[0]

    def f_off(i, j, tc):
        pos = jnp.where(tc[i] > 0, jnp.where(i % 2 == 0, j, n_f - 1 - j), 0)
        return pl.multiple_of(pos * (d_ff - tf), V7X_MXU_WIDTH)

    def window(rows_cols, offsets):
        return pl.BlockSpec((pl.Element(1), pl.Element(1)) + tuple(pl.Element(s) for s in rows_cols),
                            offsets)

    idx_spec = functools.partial(pl.BlockSpec, (None, 1, rows), memory_space=pltpu.SMEM)
    grid_spec = pltpu.PrefetchScalarGridSpec(
        num_scalar_prefetch=2,
        grid=(n_tiles, n_f),
        in_specs=[
            idx_spec(lambda i, j, te, tc: (i, 0, 0)),
            idx_spec(lambda i, j, te, tc: (jnp.minimum(i + 1, n_tiles - 1), 0, 0)),
            idx_spec(lambda i, j, te, tc: (i, 0, 0)),
            pl.BlockSpec((rows, 1), lambda i, j, te, tc: (i, 0)),
            pl.BlockSpec(memory_space=pl.ANY),
            window((d, tf), lambda i, j, te, tc: (layer, te[i], 0, f_off(i, j, tc))),
            window((d, tf), lambda i, j, te, tc: (
                layer, te[i], 0, pl.multiple_of(d_ff + f_off(i, j, tc), V7X_MXU_WIDTH))),
            window((tf, d), lambda i, j, te, tc: (layer, te[i], f_off(i, j, tc), 0)),
        ],
        out_specs=pl.BlockSpec(memory_space=pl.ANY),
        scratch_shapes=[
            pltpu.VMEM((2, rows * V7X_SUBLANES, V7X_LANES), F32),
            pltpu.VMEM((rows * V7X_SUBLANES, V7X_LANES), F32),
            pltpu.VMEM((rows, d), BF16),
            pltpu.VMEM((rows, d), F32),
            pltpu.SemaphoreType.DMA((2,)),
            pltpu.SemaphoreType.DMA((1,)),
        ],
    )
    return pl.pallas_call(
        functools.partial(_expert_kernel, rows=rows, n_f=n_f, n_tiles=n_tiles, overlap=overlap),
        grid_spec=grid_spec,
        out_shape=jax.ShapeDtypeStruct((n * TOP_K * V7X_SUBLANES, V7X_LANES), F32),
        compiler_params=_params("arbitrary", "arbitrary", vmem_limit=EXPERT_VMEM_LIMIT),
        name="routed_swiglu",
    )(tile_expert, tile_count, src, src, dst, gate, h_tm, w_gu, w_gu, w_down)


def _final_kernel(x_ref, y0_ref, y1_ref, g_ref, out_ref, *, rows):
    x = x_ref[...] + _load_token_major(y0_ref, rows) + _load_token_major(y1_ref, rows)
    out_ref[...] = _rms_unit(x) * g_ref[...]


def _final_norm(x, y, g):
    n, d = x.shape
    rows = min(PROJ_ROWS, n)
    blocks_per_slot = n // rows
    row_spec = pl.BlockSpec((rows, d), lambda i: (i, 0))
    return pl.pallas_call(
        functools.partial(_final_kernel, rows=rows),
        grid=(n // rows,),
        in_specs=[
            row_spec,
            pl.BlockSpec((rows * V7X_SUBLANES, V7X_LANES), lambda i: (i, 0)),
            pl.BlockSpec((rows * V7X_SUBLANES, V7X_LANES), lambda i: (i + blocks_per_slot, 0)),
            pl.BlockSpec((1, d), lambda i: (0, 0)),
        ],
        out_specs=row_spec,
        out_shape=jax.ShapeDtypeStruct((n, d), F32),
        compiler_params=_params("arbitrary"),
        name="final_norm",
    )(x, y, y, g.reshape(1, d))


def kernel(x, g_mix, g_ffn, g_final, a_w_in, a_conv_w, a_w_out, g_kv, w_kv, b_w_q, b_w_o,
           ffn_w_gu, ffn_w_down, moe_w_router, moe_w_gu, moe_w_down):
    batch, seq, d = x.shape
    n = batch * seq
    depth = g_mix.shape[0]
    n_self = a_w_in.shape[0]
    q_scale = LOG2_E / math.sqrt(d // N_HEADS)

    def bf(a):
        return a.astype(BF16)

    ffn_w_gu, ffn_w_down = bf(ffn_w_gu), bf(ffn_w_down)

    xs = x.reshape(n, d)
    y = None
    k = v = None
    for i in range(depth):
        o = w_o = None
        if i < n_self:
            assert y is None
            xs = _mixer(xs, g_mix[i], bf(a_w_in[i]), a_conv_w[i], bf(a_w_out[i]), batch, seq)
        else:
            j = i - n_self
            first = i == n_self
            outs = _proj(xs, g_mix[i], bf(b_w_q[j]), q_scale, y=y,
                         g_kv=g_kv if first else None, w_kv=bf(w_kv) if first else None)
            outs = list(outs)
            if y is not None:
                xs = outs.pop(0)
                y = None
            q = outs.pop(0)
            if first:
                k, v = outs
            o = _attention(q, k, v, batch, seq)
            w_o = bf(b_w_o[j])
        assert y is None
        if i % 2 == 0:
            xs = _ffn(xs, g_ffn[i], ffn_w_gu, ffn_w_down, i // 2, o=o, w_o=w_o)
        else:
            m = i // 2
            xs, h_tm, idx, w = _router(xs, g_ffn[i], moe_w_router[m], o=o, w_o=w_o)
            y = _experts(h_tm, idx, w, moe_w_gu, moe_w_down, m, n)
    assert y is not None
    return _final_norm(xs, y, g_final).reshape(batch, seq, d)
```

```python
import functools
import math

import jax
import jax.numpy as jnp
from jax import lax
from jax.experimental import pallas as pl
from jax.experimental.pallas import tpu as pltpu

F32 = jnp.float32
BF16 = jnp.bfloat16

RMS_EPS = 1e-6
LOG2_E = math.log2(math.e)
N_HEADS = 16
N_EXPERTS = 8
TOP_K = 2

V7X_LANES = 128
V7X_SUBLANES = 8
V7X_MXU_WIDTH = 256
V7X_VMEM_BYTES = 64 * 1024 * 1024
VMEM_LIMIT = V7X_VMEM_BYTES - 8 * 1024 * 1024

MIXER_ROWS = 1024
FFN_ROWS = 512
PROJ_ROWS = 1024
ROUTER_ROWS = 1024
EXPERT_ROWS = 512
ATTN_BLOCK = 256
ATTN_UNDERFLOW_LOG2 = -150.0
HEADS_PER_STEP = 2
CONV_CARRY_ROWS = V7X_SUBLANES
DMA_UNROLL = 64
EXPERT_FF_TILES = 3


def _params(*semantics):
    return pltpu.CompilerParams(dimension_semantics=semantics, vmem_limit_bytes=VMEM_LIMIT)


def _dot(a, b):
    return jnp.dot(a, b, preferred_element_type=F32)


def _rms_unit(x):
    return x * lax.rsqrt(jnp.mean(x * x, axis=-1, keepdims=True) + RMS_EPS)


def _load_token_major(ref, rows):
    parts = [ref[pl.ds(j, rows, stride=V7X_SUBLANES), :] for j in range(V7X_SUBLANES)]
    return jnp.concatenate(parts, axis=1)


def _store_token_major(ref, val, rows):
    for j in range(V7X_SUBLANES):
        ref[pl.ds(j, rows, stride=V7X_SUBLANES), :] = val[:, j * V7X_LANES:(j + 1) * V7X_LANES]


def _mixer_kernel(x_ref, g_ref, win_ref, cw_ref, wout_ref, out_ref, carry_ref, *, rows, d):
    s = pl.program_id(1)

    @pl.when(s == 0)
    def _():
        carry_ref[...] = jnp.zeros_like(carry_ref)

    x = x_ref[...]
    h = (_rms_unit(x) * g_ref[...]).astype(BF16)
    proj = _dot(h, win_ref[...])
    u = proj[:, :d] * proj[:, d:2 * d]
    b_gate = proj[:, 2 * d:]
    prev = carry_ref[...]
    row = lax.broadcasted_iota(jnp.int32, (rows, d), 0)
    u1 = jnp.where(row == 0, prev[CONV_CARRY_ROWS - 1:CONV_CARRY_ROWS, :], pltpu.roll(u, 1, 0))
    u2 = pltpu.roll(u, 2, 0)
    u2 = jnp.where(row == 0, prev[CONV_CARRY_ROWS - 2:CONV_CARRY_ROWS - 1, :], u2)
    u2 = jnp.where(row == 1, prev[CONV_CARRY_ROWS - 1:CONV_CARRY_ROWS, :], u2)
    carry_ref[...] = u[rows - CONV_CARRY_ROWS:, :]
    cw = cw_ref[...]
    conv = cw[0:1, :] * u2 + cw[1:2, :] * u1 + cw[2:3, :] * u
    mixed = (b_gate * conv).astype(BF16)
    out_ref[...] = x + _dot(mixed, wout_ref[...])


def _mixer(x, g, w_in, conv_w, w_out, batch, seq):
    n, d = x.shape
    rows = min(MIXER_ROWS, seq)
    tiles = seq // rows
    return pl.pallas_call(
        functools.partial(_mixer_kernel, rows=rows, d=d),
        grid=(batch, tiles),
        in_specs=[
            pl.BlockSpec((rows, d), lambda i, j: (i * tiles + j, 0)),
            pl.BlockSpec((1, d), lambda i, j: (0, 0)),
            pl.BlockSpec((d, 3 * d), lambda i, j: (0, 0)),
            pl.BlockSpec((3, d), lambda i, j: (0, 0)),
            pl.BlockSpec((d, d), lambda i, j: (0, 0)),
        ],
        out_specs=pl.BlockSpec((rows, d), lambda i, j: (i * tiles + j, 0)),
        out_shape=jax.ShapeDtypeStruct((n, d), F32),
        scratch_shapes=[pltpu.VMEM((CONV_CARRY_ROWS, d), F32)],
        compiler_params=_params("arbitrary", "arbitrary"),
        name="conv_mixer",
    )(x, g.reshape(1, d), w_in, conv_w, w_out)


def _ffn_kernel(*refs, has_o, d_ff):
    if has_o:
        x_ref, o_ref, wo_ref, g_ref, wgu_ref, wd_ref, out_ref = refs
    else:
        x_ref, g_ref, wgu_ref, wd_ref, out_ref = refs
    x = x_ref[...]
    if has_o:
        x = x + _dot(o_ref[...], wo_ref[...])
    h = (_rms_unit(x) * g_ref[...]).astype(BF16)
    acc = x
    for c0 in range(0, d_ff, V7X_MXU_WIDTH):
        c1 = c0 + V7X_MXU_WIDTH
        gate = _dot(h, wgu_ref[:, c0:c1])
        up = _dot(h, wgu_ref[:, d_ff + c0:d_ff + c1])
        act = (gate * jax.nn.sigmoid(gate) * up).astype(BF16)
        acc = acc + _dot(act, wd_ref[c0:c1, :])
    out_ref[...] = acc


def _ffn(x, g, w_gu, w_down, layer, o=None, w_o=None):
    n, d = x.shape
    d_ff = w_down.shape[1]
    assert d_ff % V7X_MXU_WIDTH == 0
    rows = min(FFN_ROWS, n)
    has_o = o is not None
    row_spec = pl.BlockSpec((rows, d), lambda i: (i, 0))
    once = pl.Buffered(1)
    in_specs = [row_spec]
    args = [x]
    if has_o:
        in_specs += [row_spec, pl.BlockSpec((d, d), lambda i: (0, 0), pipeline_mode=once)]
        args += [o, w_o]
    in_specs += [
        pl.BlockSpec((1, d), lambda i: (0, 0)),
        pl.BlockSpec((None, d, 2 * d_ff), lambda i: (layer, 0, 0), pipeline_mode=once),
        pl.BlockSpec((None, d_ff, d), lambda i: (layer, 0, 0), pipeline_mode=once),
    ]
    args += [g.reshape(1, d), w_gu, w_down]
    return pl.pallas_call(
        functools.partial(_ffn_kernel, has_o=has_o, d_ff=d_ff),
        grid=(n // rows,),
        in_specs=in_specs,
        out_specs=row_spec,
        out_shape=jax.ShapeDtypeStruct((n, d), F32),
        compiler_params=_params("arbitrary"),
        name="dense_swiglu",
    )(*args)


def _proj_kernel(*refs, has_y, has_kv, rows, d, q_scale):
    refs = list(refs)
    x_ref = refs.pop(0)
    if has_y:
        y0_ref = refs.pop(0)
        y1_ref = refs.pop(0)
    gq_ref = refs.pop(0)
    wq_ref = refs.pop(0)
    if has_kv:
        gkv_ref = refs.pop(0)
        wkv_ref = refs.pop(0)
    if has_y:
        xs_ref = refs.pop(0)
    q_ref = refs.pop(0)
    if has_kv:
        k_ref = refs.pop(0)
        v_ref = refs.pop(0)

    x = x_ref[...]
    if has_y:
        x = x + _load_token_major(y0_ref, rows) + _load_token_major(y1_ref, rows)
        xs_ref[...] = x
    xn = _rms_unit(x)
    q = _dot((xn * gq_ref[...]).astype(BF16), wq_ref[...])
    q_ref[...] = (q * q_scale).astype(BF16)
    if has_kv:
        kv = _dot((xn * gkv_ref[...]).astype(BF16), wkv_ref[...])
        k_ref[...] = kv[:, :d].astype(BF16)
        v_ref[...] = kv[:, d:].astype(BF16)


def _proj(x, g_q, w_q, q_scale, y=None, g_kv=None, w_kv=None):
    n, d = x.shape
    rows = min(PROJ_ROWS, n)
    has_y = y is not None
    has_kv = w_kv is not None
    row_spec = pl.BlockSpec((rows, d), lambda i: (i, 0))
    vec_spec = pl.BlockSpec((1, d), lambda i: (0, 0))
    in_specs = [row_spec]
    args = [x]
    if has_y:
        blocks_per_slot = n // rows
        tm_spec0 = pl.BlockSpec((rows * V7X_SUBLANES, V7X_LANES), lambda i: (i, 0))
        tm_spec1 = pl.BlockSpec((rows * V7X_SUBLANES, V7X_LANES),
                                lambda i: (i + blocks_per_slot, 0))
        in_specs += [tm_spec0, tm_spec1]
        args += [y, y]
    in_specs += [vec_spec, pl.BlockSpec((d, d), lambda i: (0, 0))]
    args += [g_q.reshape(1, d), w_q]
    if has_kv:
        in_specs += [vec_spec, pl.BlockSpec((d, 2 * d), lambda i: (0, 0))]
        args += [g_kv.reshape(1, d), w_kv]
    out_specs = []
    out_shape = []
    if has_y:
        out_specs.append(row_spec)
        out_shape.append(jax.ShapeDtypeStruct((n, d), F32))
    n_bf16 = 3 if has_kv else 1
    out_specs += [row_spec] * n_bf16
    out_shape += [jax.ShapeDtypeStruct((n, d), BF16)] * n_bf16
    return pl.pallas_call(
        functools.partial(_proj_kernel, has_y=has_y, has_kv=has_kv, rows=rows, d=d,
                          q_scale=q_scale),
        grid=(n // rows,),
        in_specs=in_specs,
        out_specs=out_specs,
        out_shape=out_shape,
        compiler_params=_params("arbitrary"),
        name="norm_qkv_proj",
    )(*args)


def _attn_kernel(q_ref, k_ref, v_ref, later_ref, o_ref, qs_ref, acc_ref, run_ref, *,
                 blk, dh, q_blocks):
    lane = lax.broadcasted_iota(jnp.int32, (1, HEADS_PER_STEP * dh), 1)
    head_mask = [(lane >= hh * dh) & (lane < (hh + 1) * dh) for hh in range(HEADS_PER_STEP)]
    stacked = HEADS_PER_STEP * blk

    def visit(key_blocks, diagonal_first):
        later = later_ref[...]
        qs = qs_ref[...]
        run = run_ref[...]
        total = None
        for pos, kb in enumerate(key_blocks):
            masked = diagonal_first and pos == 0
            start = pl.multiple_of(kb * blk, blk)
            k_blk = k_ref[pl.ds(start, blk), :]
            v_blk = v_ref[pl.ds(start, blk), :]
            z2 = lax.dot_general(qs, k_blk, (((1,), (1,)), ((), ())), preferred_element_type=F32)
            soft = jnp.log2(1.0 + jnp.exp2(-jnp.abs(z2)))
            log_beta = jnp.minimum(z2, 0.0) - soft
            log_1m_beta = log_beta - z2
            if masked:
                t_idx = lax.broadcasted_iota(jnp.int32, (stacked, blk), 0) & (blk - 1)
                s_idx = lax.broadcasted_iota(jnp.int32, (stacked, blk), 1)
                causal = s_idx < t_idx
                log_1m_beta = jnp.where(causal, log_1m_beta, 0.0)
            suffix = _dot(log_1m_beta.astype(BF16), later)
            attn = jnp.exp2(log_beta + suffix + run)
            if masked:
                attn = jnp.where(causal, attn, 0.0)
            av = _dot(attn.astype(BF16), v_blk)
            out = jnp.where(head_mask[0], av[:blk], av[blk:])
            total = out if total is None else total + out
            run = run + jnp.sum(log_1m_beta, axis=-1, keepdims=True)
        acc_ref[...] += total
        run_ref[...] = run

    def exhausted():
        return jnp.max(run_ref[...]) <= ATTN_UNDERFLOW_LOG2

    def q_tile(qi, carry):
        q_rows = pl.ds(pl.multiple_of(qi * blk, blk), blk)
        q = q_ref[q_rows, :]
        for hh in range(HEADS_PER_STEP):
            qs_ref[hh * blk:(hh + 1) * blk, :] = jnp.where(head_mask[hh], q, jnp.zeros_like(q))
        acc_ref[...] = jnp.zeros_like(acc_ref)
        run_ref[...] = jnp.zeros_like(run_ref)

        @pl.when(qi == 0)
        def _():
            visit([qi], True)

        @pl.when(qi > 0)
        def _():
            visit([qi, qi - 1], True)

        def pair(it, done):
            kb = qi - 2 - 2 * it

            @pl.when(jnp.logical_not(done))
            def _():
                visit([kb, kb - 1], False)

            return done | exhausted()

        done = lax.fori_loop(0, jnp.maximum(qi - 1, 0) // 2, pair, exhausted())

        @pl.when(jnp.logical_not(done) & (qi >= 2) & (qi % 2 == 0))
        def _():
            visit([0], False)

        o_ref[q_rows, :] = acc_ref[...].astype(BF16)
        return carry

    lax.fori_loop(0, q_blocks, q_tile, 0)


def _attention(q, k, v, batch, seq):
    n, d = q.shape
    dh = d // N_HEADS
    blk = min(ATTN_BLOCK, seq)
    width = HEADS_PER_STEP * dh
    q_blocks = seq // blk
    idx = jnp.arange(blk, dtype=jnp.int32)
    later = (idx[:, None] > idx[None, :]).astype(BF16)
    col_blocks = d // width
    seq_spec = pl.BlockSpec((seq, width), lambda g: (g // col_blocks, g % col_blocks))
    return pl.pallas_call(
        functools.partial(_attn_kernel, blk=blk, dh=dh, q_blocks=q_blocks),
        grid=(batch * col_blocks,),
        in_specs=[seq_spec, seq_spec, seq_spec, pl.BlockSpec((blk, blk), lambda g: (0, 0))],
        out_specs=seq_spec,
        out_shape=jax.ShapeDtypeStruct((n, d), BF16),
        scratch_shapes=[pltpu.VMEM((HEADS_PER_STEP * blk, width), BF16),
                        pltpu.VMEM((blk, width), F32),
                        pltpu.VMEM((HEADS_PER_STEP * blk, 1), F32)],
        compiler_params=_params("arbitrary"),
        name="stick_breaking_attention",
    )(q, k, v, later)


def _router_kernel(*refs, has_o, rows):
    refs = list(refs)
    x_ref = refs.pop(0)
    if has_o:
        o_ref = refs.pop(0)
        wo_ref = refs.pop(0)
    g_ref = refs.pop(0)
    wr_ref = refs.pop(0)
    if has_o:
        xs_ref = refs.pop(0)
    h_ref, idx_ref, w_ref = refs

    x = x_ref[...]
    if has_o:
        x = x + _dot(o_ref[...], wo_ref[...])
        xs_ref[...] = x
    h = _rms_unit(x) * g_ref[...]
    _store_token_major(h_ref, h, rows)

    h_hi = h.astype(BF16)
    h_lo = (h - h_hi.astype(F32)).astype(BF16)
    both = _dot(h_hi, wr_ref[...])
    logits = both[:, :V7X_LANES] + (both[:, V7X_LANES:] + _dot(h_lo, wr_ref[:, :V7X_LANES]))

    lane = lax.broadcasted_iota(jnp.int32, logits.shape, 1)
    lane_f = lane.astype(F32)
    neg_inf = jnp.float32(-jnp.inf)
    no_lane = jnp.float32(V7X_LANES)
    lg = jnp.where(lane < N_EXPERTS, logits, neg_inf)
    m1 = jnp.max(lg, axis=-1, keepdims=True)
    i1 = jnp.min(jnp.where(lg == m1, lane_f, no_lane), axis=-1, keepdims=True)
    lg2 = jnp.where(lane_f == i1, neg_inf, lg)
    m2 = jnp.max(lg2, axis=-1, keepdims=True)
    i2 = jnp.min(jnp.where(lg2 == m2, lane_f, no_lane), axis=-1, keepdims=True)
    e2 = jnp.exp(m2 - m1)
    w1 = 1.0 / (1.0 + e2)
    w2 = e2 * w1
    idx_ref[...] = jnp.where(lane == 0, i1, jnp.where(lane == 1, i2, 0.0)).astype(jnp.int32)
    w_ref[...] = jnp.where(lane == 0, w1, jnp.where(lane == 1, w2, 0.0))


def _router(x, g, w_router, o=None, w_o=None):
    n, d = x.shape
    rows = min(ROUTER_ROWS, n)
    has_o = o is not None
    wr = jnp.zeros((d, V7X_LANES), F32).at[:, :N_EXPERTS].set(w_router)
    wr_hi = wr.astype(BF16)
    wr_lo = (wr - wr_hi.astype(F32)).astype(BF16)
    wr_both = jnp.concatenate([wr_hi, wr_lo], axis=1)
    row_spec = pl.BlockSpec((rows, d), lambda i: (i, 0))
    lane_spec = pl.BlockSpec((rows, V7X_LANES), lambda i: (i, 0))
    in_specs = [row_spec]
    args = [x]
    if has_o:
        in_specs += [row_spec, pl.BlockSpec((d, d), lambda i: (0, 0))]
        args += [o, w_o]
    in_specs += [pl.BlockSpec((1, d), lambda i: (0, 0)),
                 pl.BlockSpec((d, 2 * V7X_LANES), lambda i: (0, 0))]
    args += [g.reshape(1, d), wr_both]
    out_specs = []
    out_shape = []
    if has_o:
        out_specs.append(row_spec)
        out_shape.append(jax.ShapeDtypeStruct((n, d), F32))
    out_specs += [pl.BlockSpec((rows * V7X_SUBLANES, V7X_LANES), lambda i: (i, 0)),
                  lane_spec, lane_spec]
    out_shape += [jax.ShapeDtypeStruct((n * V7X_SUBLANES, V7X_LANES), F32),
                  jax.ShapeDtypeStruct((n, V7X_LANES), jnp.int32),
                  jax.ShapeDtypeStruct((n, V7X_LANES), F32)]
    outs = pl.pallas_call(
        functools.partial(_router_kernel, has_o=has_o, rows=rows),
        grid=(n // rows,),
        in_specs=in_specs,
        out_specs=out_specs,
        out_shape=out_shape,
        compiler_params=_params("arbitrary"),
        name="router_top2",
    )(*args)
    if has_o:
        xs, h_tm, idx, w = outs
    else:
        h_tm, idx, w = outs
        xs = x
    return xs, h_tm, idx[:, :TOP_K], w[:, :TOP_K]


def _plan_routing(idx, w, rows):
    n = idx.shape[0]
    n_pairs = n * TOP_K
    n_tiles = n_pairs // rows + N_EXPERTS
    e_flat = idx.reshape(n_pairs)
    pair_id = jnp.arange(n_pairs, dtype=jnp.int32)
    _, order, w_sorted = lax.sort((e_flat, pair_id, w.reshape(n_pairs)), num_keys=1, is_stable=True)
    experts = jnp.arange(N_EXPERTS, dtype=jnp.int32)
    counts = jnp.sum((e_flat[:, None] == experts[None, :]).astype(jnp.int32), axis=0)
    starts = jnp.cumsum(counts) - counts
    pad_counts = ((counts + rows - 1) // rows) * rows
    pad_ends = jnp.cumsum(pad_counts)
    pad_starts = pad_ends - pad_counts
    tile_start = jnp.arange(n_tiles, dtype=jnp.int32) * rows
    e_of_tile = jnp.sum((tile_start[:, None] >= pad_ends[None, :]).astype(jnp.int32), axis=1)
    tile_expert = jnp.minimum(e_of_tile, N_EXPERTS - 1)
    tile_within = tile_start - pad_starts[tile_expert]
    tile_count = jnp.where(e_of_tile < N_EXPERTS,
                           jnp.clip(counts[tile_expert] - tile_within, 0, rows), 0)
    lane = jnp.arange(rows, dtype=jnp.int32)
    sorted_pos = jnp.clip((starts[tile_expert] + tile_within)[:, None] + lane[None, :],
                          0, n_pairs - 1)
    valid = lane[None, :] < tile_count[:, None]
    pair = order[sorted_pos]
    token = pair // TOP_K
    slot = pair % TOP_K
    src = jnp.where(valid, token, 0)
    dst = jnp.where(valid, slot * n + token, 0)
    gate = jnp.where(valid, w_sorted[sorted_pos], 0.0)
    return (src.reshape(n_tiles, 1, rows), dst.reshape(n_tiles, 1, rows),
            gate.reshape(n_tiles * rows, 1), tile_expert, tile_count)


def _expert_kernel(te_ref, tc_ref, src_ref, src_next_ref, dst_ref, gate_ref, h_hbm,
                   wg_ref, wu_ref, wd_ref, y_hbm,
                   gbuf_ref, ybuf_ref, hbf_ref, acc_ref, gsem, ssem, *,
                   rows, n_f, n_tiles, overlap):
    i = pl.program_id(0)
    j = pl.program_id(1)
    count = tc_ref[i]
    valid = count > 0
    full = count == rows
    first_step = j == 0
    last_step = j == n_f - 1
    has_next = (i + 1 < n_tiles) & (tc_ref[jnp.minimum(i + 1, n_tiles - 1)] > 0)
    prev_in_flight = (i > 0) & (tc_ref[jnp.maximum(i - 1, 0)] == rows)
    tile_rows = rows * V7X_SUBLANES

    def row_window(r):
        return pl.ds(pl.multiple_of(r * V7X_SUBLANES, V7X_SUBLANES), V7X_SUBLANES)

    def start_gather(idx_ref, r, slot):
        pltpu.make_async_copy(h_hbm.at[row_window(idx_ref[0, r]), :],
                              gbuf_ref.at[slot, row_window(r), :], gsem.at[slot]).start()

    def wait_gather(slot):
        pltpu.make_async_copy(h_hbm.at[pl.ds(0, tile_rows), :], gbuf_ref.at[slot],
                              gsem.at[slot]).wait()

    def scatter_copy(r):
        return pltpu.make_async_copy(ybuf_ref.at[row_window(r), :],
                                     y_hbm.at[row_window(dst_ref[0, r]), :], ssem.at[0])

    def wait_full_scatter():
        pltpu.make_async_copy(ybuf_ref, y_hbm.at[pl.ds(0, tile_rows), :], ssem.at[0]).wait()

    def for_all_rows(fn):
        def body(b, carry):
            for u in range(DMA_UNROLL):
                fn(b * DMA_UNROLL + u)
            return carry
        lax.fori_loop(0, rows // DMA_UNROLL, body, 0)

    def for_rows(n_rows, fn):
        def body(r, carry):
            fn(r)
            return carry
        lax.fori_loop(0, n_rows, body, 0)

    @pl.when(first_step & (i == 0))
    def _():
        for_all_rows(lambda r: start_gather(src_ref, r, 0))

    for slot in range(2):
        mine = valid & first_step & (i % 2 == slot)

        @pl.when(mine)
        def _(slot=slot):
            wait_gather(slot)
            hbf_ref[...] = _load_token_major(gbuf_ref.at[slot], rows).astype(BF16)
            acc_ref[...] = jnp.zeros_like(acc_ref)

        @pl.when(mine & has_next)
        def _(slot=slot):
            for_all_rows(lambda r: start_gather(src_next_ref, r, 1 - slot))

    def matmul_step(skip):
        h = hbf_ref[...]
        part = None
        width = wg_ref.shape[3]
        step = V7X_MXU_WIDTH * EXPERT_FF_TILES
        for c0 in range(skip, width, step):
            c1 = min(c0 + step, width)
            gate = _dot(h, wg_ref[0, 0, :, c0:c1].astype(BF16))
            up = _dot(h, wu_ref[0, 0, :, c0:c1].astype(BF16))
            act = (gate * jax.nn.sigmoid(gate) * up).astype(BF16)
            contrib = _dot(act, wd_ref[0, 0, c0:c1, :].astype(BF16))
            part = contrib if part is None else part + contrib
        acc_ref[...] += part

    upper = jnp.where(i % 2 == 0, j, n_f - 1 - j) == 1

    @pl.when(valid & jnp.logical_not(upper))
    def _():
        matmul_step(0)

    @pl.when(valid & upper)
    def _():
        matmul_step(overlap)

    @pl.when(valid & last_step & prev_in_flight)
    def _():
        wait_full_scatter()

    @pl.when(valid & last_step)
    def _():
        _store_token_major(ybuf_ref, acc_ref[...] * gate_ref[...], rows)

    @pl.when(valid & last_step & full)
    def _():
        for_all_rows(lambda r: scatter_copy(r).start())

    @pl.when(valid & last_step & full & jnp.logical_not(has_next))
    def _():
        wait_full_scatter()

    @pl.when(valid & last_step & jnp.logical_not(full))
    def _():
        for_rows(count, lambda r: scatter_copy(r).start())
        for_rows(count, lambda r: scatter_copy(r).wait())


def _experts(h_tm, idx, w, w_gu, w_down, layer, n):
    d = w_down.shape[3]
    d_ff = w_down.shape[2]
    rows = min(EXPERT_ROWS, n)
    assert rows % DMA_UNROLL == 0
    n_f = 2
    assert d_ff % V7X_MXU_WIDTH == 0 and d_ff // V7X_MXU_WIDTH >= n_f
    tf = -(-(d_ff // V7X_MXU_WIDTH) // n_f) * V7X_MXU_WIDTH
    overlap = n_f * tf - d_ff
    src, dst, gate, tile_expert, tile_count = _plan_routing(idx, w, rows)
    n_tiles = src.shape[0]

    def f_off(i, j, tc):
        pos = jnp.where(tc[i] > 0, jnp.where(i % 2 == 0, j, n_f - 1 - j), 0)
        return pl.multiple_of(pos * (d_ff - tf), V7X_MXU_WIDTH)

    def window(rows_cols, offsets):
        return pl.BlockSpec((pl.Element(1), pl.Element(1)) + tuple(pl.Element(s) for s in rows_cols),
                            offsets)

    idx_spec = functools.partial(pl.BlockSpec, (None, 1, rows), memory_space=pltpu.SMEM)
    grid_spec = pltpu.PrefetchScalarGridSpec(
        num_scalar_prefetch=2,
        grid=(n_tiles, n_f),
        in_specs=[
            idx_spec(lambda i, j, te, tc: (i, 0, 0)),
            idx_spec(lambda i, j, te, tc: (jnp.minimum(i + 1, n_tiles - 1), 0, 0)),
            idx_spec(lambda i, j, te, tc: (i, 0, 0)),
            pl.BlockSpec((rows, 1), lambda i, j, te, tc: (i, 0)),
            pl.BlockSpec(memory_space=pl.ANY),
            window((d, tf), lambda i, j, te, tc: (layer, te[i], 0, f_off(i, j, tc))),
            window((d, tf), lambda i, j, te, tc: (
                layer, te[i], 0, pl.multiple_of(d_ff + f_off(i, j, tc), V7X_MXU_WIDTH))),
            window((tf, d), lambda i, j, te, tc: (layer, te[i], f_off(i, j, tc), 0)),
        ],
        out_specs=pl.BlockSpec(memory_space=pl.ANY),
        scratch_shapes=[
            pltpu.VMEM((2, rows * V7X_SUBLANES, V7X_LANES), F32),
            pltpu.VMEM((rows * V7X_SUBLANES, V7X_LANES), F32),
            pltpu.VMEM((rows, d), BF16),
            pltpu.VMEM((rows, d), F32),
            pltpu.SemaphoreType.DMA((2,)),
            pltpu.SemaphoreType.DMA((1,)),
        ],
    )
    return pl.pallas_call(
        functools.partial(_expert_kernel, rows=rows, n_f=n_f, n_tiles=n_tiles, overlap=overlap),
        grid_spec=grid_spec,
        out_shape=jax.ShapeDtypeStruct((n * TOP_K * V7X_SUBLANES, V7X_LANES), F32),
        compiler_params=_params("arbitrary", "arbitrary"),
        name="routed_swiglu",
    )(tile_expert, tile_count, src, src, dst, gate, h_tm, w_gu, w_gu, w_down)


def _final_kernel(x_ref, y0_ref, y1_ref, g_ref, out_ref, *, rows):
    x = x_ref[...] + _load_token_major(y0_ref, rows) + _load_token_major(y1_ref, rows)
    out_ref[...] = _rms_unit(x) * g_ref[...]


def _final_norm(x, y, g):
    n, d = x.shape
    rows = min(PROJ_ROWS, n)
    blocks_per_slot = n // rows
    row_spec = pl.BlockSpec((rows, d), lambda i: (i, 0))
    return pl.pallas_call(
        functools.partial(_final_kernel, rows=rows),
        grid=(n // rows,),
        in_specs=[
            row_spec,
            pl.BlockSpec((rows * V7X_SUBLANES, V7X_LANES), lambda i: (i, 0)),
            pl.BlockSpec((rows * V7X_SUBLANES, V7X_LANES), lambda i: (i + blocks_per_slot, 0)),
            pl.BlockSpec((1, d), lambda i: (0, 0)),
        ],
        out_specs=row_spec,
        out_shape=jax.ShapeDtypeStruct((n, d), F32),
        compiler_params=_params("arbitrary"),
        name="final_norm",
    )(x, y, y, g.reshape(1, d))


def kernel(x, g_mix, g_ffn, g_final, a_w_in, a_conv_w, a_w_out, g_kv, w_kv, b_w_q, b_w_o,
           ffn_w_gu, ffn_w_down, moe_w_router, moe_w_gu, moe_w_down):
    batch, seq, d = x.shape
    n = batch * seq
    depth = g_mix.shape[0]
    n_self = a_w_in.shape[0]
    q_scale = LOG2_E / math.sqrt(d // N_HEADS)

    def bf(a):
        return a.astype(BF16)

    ffn_w_gu, ffn_w_down = bf(ffn_w_gu), bf(ffn_w_down)

    xs = x.reshape(n, d)
    y = None
    k = v = None
    for i in range(depth):
        o = w_o = None
        if i < n_self:
            assert y is None
            xs = _mixer(xs, g_mix[i], bf(a_w_in[i]), a_conv_w[i], bf(a_w_out[i]), batch, seq)
        else:
            j = i - n_self
            first = i == n_self
            outs = _proj(xs, g_mix[i], bf(b_w_q[j]), q_scale, y=y,
                         g_kv=g_kv if first else None, w_kv=bf(w_kv) if first else None)
            outs = list(outs)
            if y is not None:
                xs = outs.pop(0)
                y = None
            q = outs.pop(0)
            if first:
                k, v = outs
            o = _attention(q, k, v, batch, seq)
            w_o = bf(b_w_o[j])
        assert y is None
        if i % 2 == 0:
            xs = _ffn(xs, g_ffn[i], ffn_w_gu, ffn_w_down, i // 2, o=o, w_o=w_o)
        else:
            m = i // 2
            xs, h_tm, idx, w = _router(xs, g_ffn[i], moe_w_router[m], o=o, w_o=w_o)
            y = _experts(h_tm, idx, w, moe_w_gu, moe_w_down, m, n)
    assert y is not None
    return _final_norm(xs, y, g_final).reshape(batch, seq, d)
```

```python
import functools
import math

import jax
import jax.numpy as jnp
from jax import lax
from jax.experimental import pallas as pl
from jax.experimental.pallas import tpu as pltpu

F32 = jnp.float32
BF16 = jnp.bfloat16

RMS_EPS = 1e-6
LOG2_E = math.log2(math.e)
N_HEADS = 16
N_EXPERTS = 8
TOP_K = 2

V7X_LANES = 128
V7X_SUBLANES = 8
V7X_MXU_WIDTH = 256
V7X_VMEM_BYTES = 64 * 1024 * 1024
VMEM_LIMIT = V7X_VMEM_BYTES - 8 * 1024 * 1024

MIXER_ROWS = 1024
FFN_ROWS = 512
PROJ_ROWS = 1024
ROUTER_ROWS = 1024
EXPERT_ROWS = 512
ATTN_BLOCK = 256
ATTN_UNDERFLOW_LOG2 = -150.0
HEADS_PER_STEP = 2
CONV_CARRY_ROWS = V7X_SUBLANES
DMA_UNROLL = 64
EXPERT_FF_TILES = 3


def _params(*semantics):
    return pltpu.CompilerParams(dimension_semantics=semantics, vmem_limit_bytes=VMEM_LIMIT)


def _dot(a, b):
    return jnp.dot(a, b, preferred_element_type=F32)


def _rms_unit(x):
    return x * lax.rsqrt(jnp.mean(x * x, axis=-1, keepdims=True) + RMS_EPS)


def _load_token_major(ref, rows):
    parts = [ref[pl.ds(j, rows, stride=V7X_SUBLANES), :] for j in range(V7X_SUBLANES)]
    return jnp.concatenate(parts, axis=1)


def _store_token_major(ref, val, rows):
    for j in range(V7X_SUBLANES):
        ref[pl.ds(j, rows, stride=V7X_SUBLANES), :] = val[:, j * V7X_LANES:(j + 1) * V7X_LANES]


def _mixer_kernel(x_ref, g_ref, win_ref, cw_ref, wout_ref, out_ref, carry_ref, *, rows, d):
    s = pl.program_id(1)

    @pl.when(s == 0)
    def _():
        carry_ref[...] = jnp.zeros_like(carry_ref)

    x = x_ref[...]
    h = (_rms_unit(x) * g_ref[...]).astype(BF16)
    proj = _dot(h, win_ref[...])
    u = proj[:, :d] * proj[:, d:2 * d]
    b_gate = proj[:, 2 * d:]
    prev = carry_ref[...]
    row = lax.broadcasted_iota(jnp.int32, (rows, d), 0)
    u1 = jnp.where(row == 0, prev[CONV_CARRY_ROWS - 1:CONV_CARRY_ROWS, :], pltpu.roll(u, 1, 0))
    u2 = pltpu.roll(u, 2, 0)
    u2 = jnp.where(row == 0, prev[CONV_CARRY_ROWS - 2:CONV_CARRY_ROWS - 1, :], u2)
    u2 = jnp.where(row == 1, prev[CONV_CARRY_ROWS - 1:CONV_CARRY_ROWS, :], u2)
    carry_ref[...] = u[rows - CONV_CARRY_ROWS:, :]
    cw = cw_ref[...]
    conv = cw[0:1, :] * u2 + cw[1:2, :] * u1 + cw[2:3, :] * u
    mixed = (b_gate * conv).astype(BF16)
    out_ref[...] = x + _dot(mixed, wout_ref[...])


def _mixer(x, g, w_in, conv_w, w_out, batch, seq):
    n, d = x.shape
    rows = min(MIXER_ROWS, seq)
    tiles = seq // rows
    return pl.pallas_call(
        functools.partial(_mixer_kernel, rows=rows, d=d),
        grid=(batch, tiles),
        in_specs=[
            pl.BlockSpec((rows, d), lambda i, j: (i * tiles + j, 0)),
            pl.BlockSpec((1, d), lambda i, j: (0, 0)),
            pl.BlockSpec((d, 3 * d), lambda i, j: (0, 0)),
            pl.BlockSpec((3, d), lambda i, j: (0, 0)),
            pl.BlockSpec((d, d), lambda i, j: (0, 0)),
        ],
        out_specs=pl.BlockSpec((rows, d), lambda i, j: (i * tiles + j, 0)),
        out_shape=jax.ShapeDtypeStruct((n, d), F32),
        scratch_shapes=[pltpu.VMEM((CONV_CARRY_ROWS, d), F32)],
        compiler_params=_params("arbitrary", "arbitrary"),
        name="conv_mixer",
    )(x, g.reshape(1, d), w_in, conv_w, w_out)


def _ffn_kernel(*refs, has_o, d_ff):
    if has_o:
        x_ref, o_ref, wo_ref, g_ref, wgu_ref, wd_ref, out_ref = refs
    else:
        x_ref, g_ref, wgu_ref, wd_ref, out_ref = refs
    x = x_ref[...]
    if has_o:
        x = x + _dot(o_ref[...], wo_ref[...])
    h = (_rms_unit(x) * g_ref[...]).astype(BF16)
    acc = x
    for c0 in range(0, d_ff, V7X_MXU_WIDTH):
        c1 = c0 + V7X_MXU_WIDTH
        gate = _dot(h, wgu_ref[:, c0:c1])
        up = _dot(h, wgu_ref[:, d_ff + c0:d_ff + c1])
        act = (gate * jax.nn.sigmoid(gate) * up).astype(BF16)
        acc = acc + _dot(act, wd_ref[c0:c1, :])
    out_ref[...] = acc


def _ffn(x, g, w_gu, w_down, layer, o=None, w_o=None):
    n, d = x.shape
    d_ff = w_down.shape[1]
    assert d_ff % V7X_MXU_WIDTH == 0
    rows = min(FFN_ROWS, n)
    has_o = o is not None
    row_spec = pl.BlockSpec((rows, d), lambda i: (i, 0))
    once = pl.Buffered(1)
    in_specs = [row_spec]
    args = [x]
    if has_o:
        in_specs += [row_spec, pl.BlockSpec((d, d), lambda i: (0, 0), pipeline_mode=once)]
        args += [o, w_o]
    in_specs += [
        pl.BlockSpec((1, d), lambda i: (0, 0)),
        pl.BlockSpec((None, d, 2 * d_ff), lambda i: (layer, 0, 0), pipeline_mode=once),
        pl.BlockSpec((None, d_ff, d), lambda i: (layer, 0, 0), pipeline_mode=once),
    ]
    args += [g.reshape(1, d), w_gu, w_down]
    return pl.pallas_call(
        functools.partial(_ffn_kernel, has_o=has_o, d_ff=d_ff),
        grid=(n // rows,),
        in_specs=in_specs,
        out_specs=row_spec,
        out_shape=jax.ShapeDtypeStruct((n, d), F32),
        compiler_params=_params("arbitrary"),
        name="dense_swiglu",
    )(*args)


def _proj_kernel(*refs, has_y, has_kv, rows, d, q_scale):
    refs = list(refs)
    x_ref = refs.pop(0)
    if has_y:
        y0_ref = refs.pop(0)
        y1_ref = refs.pop(0)
    gq_ref = refs.pop(0)
    wq_ref = refs.pop(0)
    if has_kv:
        gkv_ref = refs.pop(0)
        wkv_ref = refs.pop(0)
    if has_y:
        xs_ref = refs.pop(0)
    q_ref = refs.pop(0)
    if has_kv:
        k_ref = refs.pop(0)
        v_ref = refs.pop(0)

    x = x_ref[...]
    if has_y:
        x = x + _load_token_major(y0_ref, rows) + _load_token_major(y1_ref, rows)
        xs_ref[...] = x
    xn = _rms_unit(x)
    q = _dot((xn * gq_ref[...]).astype(BF16), wq_ref[...])
    q_ref[...] = (q * q_scale).astype(BF16)
    if has_kv:
        kv = _dot((xn * gkv_ref[...]).astype(BF16), wkv_ref[...])
        k_ref[...] = kv[:, :d].astype(BF16)
        v_ref[...] = kv[:, d:].astype(BF16)


def _proj(x, g_q, w_q, q_scale, y=None, g_kv=None, w_kv=None):
    n, d = x.shape
    rows = min(PROJ_ROWS, n)
    has_y = y is not None
    has_kv = w_kv is not None
    row_spec = pl.BlockSpec((rows, d), lambda i: (i, 0))
    vec_spec = pl.BlockSpec((1, d), lambda i: (0, 0))
    in_specs = [row_spec]
    args = [x]
    if has_y:
        blocks_per_slot = n // rows
        tm_spec0 = pl.BlockSpec((rows * V7X_SUBLANES, V7X_LANES), lambda i: (i, 0))
        tm_spec1 = pl.BlockSpec((rows * V7X_SUBLANES, V7X_LANES),
                                lambda i: (i + blocks_per_slot, 0))
        in_specs += [tm_spec0, tm_spec1]
        args += [y, y]
    in_specs += [vec_spec, pl.BlockSpec((d, d), lambda i: (0, 0))]
    args += [g_q.reshape(1, d), w_q]
    if has_kv:
        in_specs += [vec_spec, pl.BlockSpec((d, 2 * d), lambda i: (0, 0))]
        args += [g_kv.reshape(1, d), w_kv]
    out_specs = []
    out_shape = []
    if has_y:
        out_specs.append(row_spec)
        out_shape.append(jax.ShapeDtypeStruct((n, d), F32))
    n_bf16 = 3 if has_kv else 1
    out_specs += [row_spec] * n_bf16
    out_shape += [jax.ShapeDtypeStruct((n, d), BF16)] * n_bf16
    return pl.pallas_call(
        functools.partial(_proj_kernel, has_y=has_y, has_kv=has_kv, rows=rows, d=d,
                          q_scale=q_scale),
        grid=(n // rows,),
        in_specs=in_specs,
        out_specs=out_specs,
        out_shape=out_shape,
        compiler_params=_params("arbitrary"),
        name="norm_qkv_proj",
    )(*args)


def _attn_kernel(q_ref, k_ref, v_ref, later_ref, o_ref, qs_ref, acc_ref, run_ref, *,
                 blk, dh, q_blocks):
    lane = lax.broadcasted_iota(jnp.int32, (1, HEADS_PER_STEP * dh), 1)
    head_mask = [(lane >= hh * dh) & (lane < (hh + 1) * dh) for hh in range(HEADS_PER_STEP)]
    stacked = HEADS_PER_STEP * blk

    def visit(work):
        later = later_ref[...]
        runs = {}
        totals = {}
        for slot, kb, masked in work:
            run = runs[slot] if slot in runs else run_ref[slot]
            start = pl.multiple_of(kb * blk, blk)
            k_blk = k_ref[pl.ds(start, blk), :]
            v_blk = v_ref[pl.ds(start, blk), :]
            z2 = lax.dot_general(qs_ref[slot], k_blk, (((1,), (1,)), ((), ())),
                                 preferred_element_type=F32)
            soft = jnp.log2(1.0 + jnp.exp2(-jnp.abs(z2)))
            log_beta = jnp.minimum(z2, 0.0) - soft
            log_1m_beta = log_beta - z2
            if masked:
                t_idx = lax.broadcasted_iota(jnp.int32, (stacked, blk), 0) & (blk - 1)
                s_idx = lax.broadcasted_iota(jnp.int32, (stacked, blk), 1)
                causal = s_idx < t_idx
                log_1m_beta = jnp.where(causal, log_1m_beta, 0.0)
            suffix = _dot(log_1m_beta.astype(BF16), later)
            attn = jnp.exp2(log_beta + suffix + run)
            if masked:
                attn = jnp.where(causal, attn, 0.0)
            av = _dot(attn.astype(BF16), v_blk)
            out = jnp.where(head_mask[0], av[:blk], av[blk:])
            totals[slot] = out if slot not in totals else totals[slot] + out
            runs[slot] = run + jnp.sum(log_1m_beta, axis=-1, keepdims=True)
        for slot in totals:
            acc_ref[slot] += totals[slot]
            run_ref[slot] = runs[slot]

    def exhausted(slot):
        return jnp.max(run_ref[slot]) <= ATTN_UNDERFLOW_LOG2

    def load_tile(slot, qi):
        q = q_ref[pl.ds(pl.multiple_of(qi * blk, blk), blk), :]
        for hh in range(HEADS_PER_STEP):
            qs_ref[slot, hh * blk:(hh + 1) * blk, :] = jnp.where(head_mask[hh], q, jnp.zeros_like(q))
        acc_ref[slot] = jnp.zeros_like(acc_ref[slot])
        run_ref[slot] = jnp.zeros_like(run_ref[slot])

    def finish_tile(slot, qi, done):
        def pair(it, done):
            kb = qi - 2 - 2 * it

            @pl.when(jnp.logical_not(done))
            def _():
                visit([(slot, kb, False), (slot, kb - 1, False)])

            return done | exhausted(slot)

        done = lax.fori_loop(0, jnp.maximum(qi - 1, 0) // 2, pair, done)

        @pl.when(jnp.logical_not(done) & (qi >= 2) & (qi % 2 == 0))
        def _():
            visit([(slot, 0, False)])

        o_ref[pl.ds(pl.multiple_of(qi * blk, blk), blk), :] = acc_ref[slot].astype(BF16)

    def two_tiles(m, carry):
        qa = 2 * m
        qb = qa + 1
        load_tile(0, qa)
        load_tile(1, qb)

        @pl.when(m == 0)
        def _():
            visit([(0, qa, True), (1, qb, True), (1, qb - 1, False)])

        @pl.when(m > 0)
        def _():
            visit([(0, qa, True), (1, qb, True), (0, qa - 1, False), (1, qb - 1, False)])

        finish_tile(0, qa, exhausted(0))
        finish_tile(1, qb, exhausted(1))
        return carry

    lax.fori_loop(0, q_blocks // 2, two_tiles, 0)


def _attention(q, k, v, batch, seq):
    n, d = q.shape
    dh = d // N_HEADS
    blk = min(ATTN_BLOCK, seq)
    width = HEADS_PER_STEP * dh
    q_blocks = seq // blk
    assert q_blocks % 2 == 0
    idx = jnp.arange(blk, dtype=jnp.int32)
    later = (idx[:, None] > idx[None, :]).astype(BF16)
    col_blocks = d // width
    seq_spec = pl.BlockSpec((seq, width), lambda g: (g // col_blocks, g % col_blocks))
    return pl.pallas_call(
        functools.partial(_attn_kernel, blk=blk, dh=dh, q_blocks=q_blocks),
        grid=(batch * col_blocks,),
        in_specs=[seq_spec, seq_spec, seq_spec, pl.BlockSpec((blk, blk), lambda g: (0, 0))],
        out_specs=seq_spec,
        out_shape=jax.ShapeDtypeStruct((n, d), BF16),
        scratch_shapes=[pltpu.VMEM((2, HEADS_PER_STEP * blk, width), BF16),
                        pltpu.VMEM((2, blk, width), F32),
                        pltpu.VMEM((2, HEADS_PER_STEP * blk, 1), F32)],
        compiler_params=_params("arbitrary"),
        name="stick_breaking_attention",
    )(q, k, v, later)


def _router_kernel(*refs, has_o, rows):
    refs = list(refs)
    x_ref = refs.pop(0)
    if has_o:
        o_ref = refs.pop(0)
        wo_ref = refs.pop(0)
    g_ref = refs.pop(0)
    wr_ref = refs.pop(0)
    if has_o:
        xs_ref = refs.pop(0)
    h_ref, idx_ref, w_ref = refs

    x = x_ref[...]
    if has_o:
        x = x + _dot(o_ref[...], wo_ref[...])
        xs_ref[...] = x
    h = _rms_unit(x) * g_ref[...]
    _store_token_major(h_ref, h, rows)

    h_hi = h.astype(BF16)
    h_lo = (h - h_hi.astype(F32)).astype(BF16)
    both = _dot(h_hi, wr_ref[...])
    logits = both[:, :V7X_LANES] + (both[:, V7X_LANES:] + _dot(h_lo, wr_ref[:, :V7X_LANES]))

    lane = lax.broadcasted_iota(jnp.int32, logits.shape, 1)
    lane_f = lane.astype(F32)
    neg_inf = jnp.float32(-jnp.inf)
    no_lane = jnp.float32(V7X_LANES)
    lg = jnp.where(lane < N_EXPERTS, logits, neg_inf)
    m1 = jnp.max(lg, axis=-1, keepdims=True)
    i1 = jnp.min(jnp.where(lg == m1, lane_f, no_lane), axis=-1, keepdims=True)
    lg2 = jnp.where(lane_f == i1, neg_inf, lg)
    m2 = jnp.max(lg2, axis=-1, keepdims=True)
    i2 = jnp.min(jnp.where(lg2 == m2, lane_f, no_lane), axis=-1, keepdims=True)
    e2 = jnp.exp(m2 - m1)
    w1 = 1.0 / (1.0 + e2)
    w2 = e2 * w1
    idx_ref[...] = jnp.where(lane == 0, i1, jnp.where(lane == 1, i2, 0.0)).astype(jnp.int32)
    w_ref[...] = jnp.where(lane == 0, w1, jnp.where(lane == 1, w2, 0.0))


def _router(x, g, w_router, o=None, w_o=None):
    n, d = x.shape
    rows = min(ROUTER_ROWS, n)
    has_o = o is not None
    wr = jnp.zeros((d, V7X_LANES), F32).at[:, :N_EXPERTS].set(w_router)
    wr_hi = wr.astype(BF16)
    wr_lo = (wr - wr_hi.astype(F32)).astype(BF16)
    wr_both = jnp.concatenate([wr_hi, wr_lo], axis=1)
    row_spec = pl.BlockSpec((rows, d), lambda i: (i, 0))
    lane_spec = pl.BlockSpec((rows, V7X_LANES), lambda i: (i, 0))
    in_specs = [row_spec]
    args = [x]
    if has_o:
        in_specs += [row_spec, pl.BlockSpec((d, d), lambda i: (0, 0))]
        args += [o, w_o]
    in_specs += [pl.BlockSpec((1, d), lambda i: (0, 0)),
                 pl.BlockSpec((d, 2 * V7X_LANES), lambda i: (0, 0))]
    args += [g.reshape(1, d), wr_both]
    out_specs = []
    out_shape = []
    if has_o:
        out_specs.append(row_spec)
        out_shape.append(jax.ShapeDtypeStruct((n, d), F32))
    out_specs += [pl.BlockSpec((rows * V7X_SUBLANES, V7X_LANES), lambda i: (i, 0)),
                  lane_spec, lane_spec]
    out_shape += [jax.ShapeDtypeStruct((n * V7X_SUBLANES, V7X_LANES), F32),
                  jax.ShapeDtypeStruct((n, V7X_LANES), jnp.int32),
                  jax.ShapeDtypeStruct((n, V7X_LANES), F32)]
    outs = pl.pallas_call(
        functools.partial(_router_kernel, has_o=has_o, rows=rows),
        grid=(n // rows,),
        in_specs=in_specs,
        out_specs=out_specs,
        out_shape=out_shape,
        compiler_params=_params("arbitrary"),
        name="router_top2",
    )(*args)
    if has_o:
        xs, h_tm, idx, w = outs
    else:
        h_tm, idx, w = outs
        xs = x
    return xs, h_tm, idx[:, :TOP_K], w[:, :TOP_K]


def _plan_routing(idx, w, rows):
    n = idx.shape[0]
    n_pairs = n * TOP_K
    n_tiles = n_pairs // rows + N_EXPERTS
    e_flat = idx.reshape(n_pairs)
    pair_id = jnp.arange(n_pairs, dtype=jnp.int32)
    _, order, w_sorted = lax.sort((e_flat, pair_id, w.reshape(n_pairs)), num_keys=1, is_stable=True)
    experts = jnp.arange(N_EXPERTS, dtype=jnp.int32)
    counts = jnp.sum((e_flat[:, None] == experts[None, :]).astype(jnp.int32), axis=0)
    starts = jnp.cumsum(counts) - counts
    pad_counts = ((counts + rows - 1) // rows) * rows
    pad_ends = jnp.cumsum(pad_counts)
    pad_starts = pad_ends - pad_counts
    tile_start = jnp.arange(n_tiles, dtype=jnp.int32) * rows
    e_of_tile = jnp.sum((tile_start[:, None] >= pad_ends[None, :]).astype(jnp.int32), axis=1)
    tile_expert = jnp.minimum(e_of_tile, N_EXPERTS - 1)
    tile_within = tile_start - pad_starts[tile_expert]
    tile_count = jnp.where(e_of_tile < N_EXPERTS,
                           jnp.clip(counts[tile_expert] - tile_within, 0, rows), 0)
    lane = jnp.arange(rows, dtype=jnp.int32)
    sorted_pos = jnp.clip((starts[tile_expert] + tile_within)[:, None] + lane[None, :],
                          0, n_pairs - 1)
    valid = lane[None, :] < tile_count[:, None]
    pair = order[sorted_pos]
    token = pair // TOP_K
    slot = pair % TOP_K
    src = jnp.where(valid, token, 0)
    dst = jnp.where(valid, slot * n + token, 0)
    gate = jnp.where(valid, w_sorted[sorted_pos], 0.0)
    return (src.reshape(n_tiles, 1, rows), dst.reshape(n_tiles, 1, rows),
            gate.reshape(n_tiles * rows, 1), tile_expert, tile_count)


def _expert_kernel(te_ref, tc_ref, src_ref, src_next_ref, dst_ref, gate_ref, h_hbm,
                   wg_ref, wu_ref, wd_ref, y_hbm,
                   gbuf_ref, ybuf_ref, hbf_ref, acc_ref, gsem, ssem, *,
                   rows, n_f, n_tiles, overlap):
    i = pl.program_id(0)
    j = pl.program_id(1)
    count = tc_ref[i]
    valid = count > 0
    full = count == rows
    first_step = j == 0
    last_step = j == n_f - 1
    has_next = (i + 1 < n_tiles) & (tc_ref[jnp.minimum(i + 1, n_tiles - 1)] > 0)
    prev_in_flight = (i > 0) & (tc_ref[jnp.maximum(i - 1, 0)] == rows)
    tile_rows = rows * V7X_SUBLANES

    def row_window(r):
        return pl.ds(pl.multiple_of(r * V7X_SUBLANES, V7X_SUBLANES), V7X_SUBLANES)

    def start_gather(idx_ref, r, slot):
        pltpu.make_async_copy(h_hbm.at[row_window(idx_ref[0, r]), :],
                              gbuf_ref.at[slot, row_window(r), :], gsem.at[slot]).start()

    def wait_gather(slot):
        pltpu.make_async_copy(h_hbm.at[pl.ds(0, tile_rows), :], gbuf_ref.at[slot],
                              gsem.at[slot]).wait()

    def scatter_copy(r):
        return pltpu.make_async_copy(ybuf_ref.at[row_window(r), :],
                                     y_hbm.at[row_window(dst_ref[0, r]), :], ssem.at[0])

    def wait_full_scatter():
        pltpu.make_async_copy(ybuf_ref, y_hbm.at[pl.ds(0, tile_rows), :], ssem.at[0]).wait()

    def for_all_rows(fn):
        def body(b, carry):
            for u in range(DMA_UNROLL):
                fn(b * DMA_UNROLL + u)
            return carry
        lax.fori_loop(0, rows // DMA_UNROLL, body, 0)

    def for_rows(n_rows, fn):
        def body(r, carry):
            fn(r)
            return carry
        lax.fori_loop(0, n_rows, body, 0)

    @pl.when(first_step & (i == 0))
    def _():
        for_all_rows(lambda r: start_gather(src_ref, r, 0))

    for slot in range(2):
        mine = valid & first_step & (i % 2 == slot)

        @pl.when(mine)
        def _(slot=slot):
            wait_gather(slot)
            hbf_ref[...] = _load_token_major(gbuf_ref.at[slot], rows).astype(BF16)
            acc_ref[...] = jnp.zeros_like(acc_ref)

        @pl.when(mine & has_next)
        def _(slot=slot):
            for_all_rows(lambda r: start_gather(src_next_ref, r, 1 - slot))

    def matmul_step(skip):
        h = hbf_ref[...]
        part = None
        width = wg_ref.shape[3]
        step = V7X_MXU_WIDTH * EXPERT_FF_TILES
        for c0 in range(skip, width, step):
            c1 = min(c0 + step, width)
            gate = _dot(h, wg_ref[0, 0, :, c0:c1].astype(BF16))
            up = _dot(h, wu_ref[0, 0, :, c0:c1].astype(BF16))
            act = (gate * jax.nn.sigmoid(gate) * up).astype(BF16)
            contrib = _dot(act, wd_ref[0, 0, c0:c1, :].astype(BF16))
            part = contrib if part is None else part + contrib
        acc_ref[...] += part

    upper = jnp.where(i % 2 == 0, j, n_f - 1 - j) == 1

    @pl.when(valid & jnp.logical_not(upper))
    def _():
        matmul_step(0)

    @pl.when(valid & upper)
    def _():
        matmul_step(overlap)

    @pl.when(valid & last_step & prev_in_flight)
    def _():
        wait_full_scatter()

    @pl.when(valid & last_step)
    def _():
        _store_token_major(ybuf_ref, acc_ref[...] * gate_ref[...], rows)

    @pl.when(valid & last_step & full)
    def _():
        for_all_rows(lambda r: scatter_copy(r).start())

    @pl.when(valid & last_step & full & jnp.logical_not(has_next))
    def _():
        wait_full_scatter()

    @pl.when(valid & last_step & jnp.logical_not(full))
    def _():
        for_rows(count, lambda r: scatter_copy(r).start())
        for_rows(count, lambda r: scatter_copy(r).wait())


def _experts(h_tm, idx, w, w_gu, w_down, layer, n):
    d = w_down.shape[3]
    d_ff = w_down.shape[2]
    rows = min(EXPERT_ROWS, n)
    assert rows % DMA_UNROLL == 0
    n_f = 2
    assert d_ff % V7X_MXU_WIDTH == 0 and d_ff // V7X_MXU_WIDTH >= n_f
    tf = -(-(d_ff // V7X_MXU_WIDTH) // n_f) * V7X_MXU_WIDTH
    overlap = n_f * tf - d_ff
    src, dst, gate, tile_expert, tile_count = _plan_routing(idx, w, rows)
    n_tiles = src.shape[0]

    def f_off(i, j, tc):
        pos = jnp.where(tc[i] > 0, jnp.where(i % 2 == 0, j, n_f - 1 - j), 0)
        return pl.multiple_of(pos * (d_ff - tf), V7X_MXU_WIDTH)

    def window(rows_cols, offsets):
        return pl.BlockSpec((pl.Element(1), pl.Element(1)) + tuple(pl.Element(s) for s in rows_cols),
                            offsets)

    idx_spec = functools.partial(pl.BlockSpec, (None, 1, rows), memory_space=pltpu.SMEM)
    grid_spec = pltpu.PrefetchScalarGridSpec(
        num_scalar_prefetch=2,
        grid=(n_tiles, n_f),
        in_specs=[
            idx_spec(lambda i, j, te, tc: (i, 0, 0)),
            idx_spec(lambda i, j, te, tc: (jnp.minimum(i + 1, n_tiles - 1), 0, 0)),
            idx_spec(lambda i, j, te, tc: (i, 0, 0)),
            pl.BlockSpec((rows, 1), lambda i, j, te, tc: (i, 0)),
            pl.BlockSpec(memory_space=pl.ANY),
            window((d, tf), lambda i, j, te, tc: (layer, te[i], 0, f_off(i, j, tc))),
            window((d, tf), lambda i, j, te, tc: (
                layer, te[i], 0, pl.multiple_of(d_ff + f_off(i, j, tc), V7X_MXU_WIDTH))),
            window((tf, d), lambda i, j, te, tc: (layer, te[i], f_off(i, j, tc), 0)),
        ],
        out_specs=pl.BlockSpec(memory_space=pl.ANY),
        scratch_shapes=[
            pltpu.VMEM((2, rows * V7X_SUBLANES, V7X_LANES), F32),
            pltpu.VMEM((rows * V7X_SUBLANES, V7X_LANES), F32),
            pltpu.VMEM((rows, d), BF16),
            pltpu.VMEM((rows, d), F32),
            pltpu.SemaphoreType.DMA((2,)),
            pltpu.SemaphoreType.DMA((1,)),
        ],
    )
    return pl.pallas_call(
        functools.partial(_expert_kernel, rows=rows, n_f=n_f, n_tiles=n_tiles, overlap=overlap),
        grid_spec=grid_spec,
        out_shape=jax.ShapeDtypeStruct((n * TOP_K * V7X_SUBLANES, V7X_LANES), F32),
        compiler_params=_params("arbitrary", "arbitrary"),
        name="routed_swiglu",
    )(tile_expert, tile_count, src, src, dst, gate, h_tm, w_gu, w_gu, w_down)


def _final_kernel(x_ref, y0_ref, y1_ref, g_ref, out_ref, *, rows):
    x = x_ref[...] + _load_token_major(y0_ref, rows) + _load_token_major(y1_ref, rows)
    out_ref[...] = _rms_unit(x) * g_ref[...]


def _final_norm(x, y, g):
    n, d = x.shape
    rows = min(PROJ_ROWS, n)
    blocks_per_slot = n // rows
    row_spec = pl.BlockSpec((rows, d), lambda i: (i, 0))
    return pl.pallas_call(
        functools.partial(_final_kernel, rows=rows),
        grid=(n // rows,),
        in_specs=[
            row_spec,
            pl.BlockSpec((rows * V7X_SUBLANES, V7X_LANES), lambda i: (i, 0)),
            pl.BlockSpec((rows * V7X_SUBLANES, V7X_LANES), lambda i: (i + blocks_per_slot, 0)),
            pl.BlockSpec((1, d), lambda i: (0, 0)),
        ],
        out_specs=row_spec,
        out_shape=jax.ShapeDtypeStruct((n, d), F32),
        compiler_params=_params("arbitrary"),
        name="final_norm",
    )(x, y, y, g.reshape(1, d))


def kernel(x, g_mix, g_ffn, g_final, a_w_in, a_conv_w, a_w_out, g_kv, w_kv, b_w_q, b_w_o,
           ffn_w_gu, ffn_w_down, moe_w_router, moe_w_gu, moe_w_down):
    batch, seq, d = x.shape
    n = batch * seq
    depth = g_mix.shape[0]
    n_self = a_w_in.shape[0]
    q_scale = LOG2_E / math.sqrt(d // N_HEADS)

    def bf(a):
        return a.astype(BF16)

    ffn_w_gu, ffn_w_down = bf(ffn_w_gu), bf(ffn_w_down)

    xs = x.reshape(n, d)
    y = None
    k = v = None
    for i in range(depth):
        o = w_o = None
        if i < n_self:
            assert y is None
            xs = _mixer(xs, g_mix[i], bf(a_w_in[i]), a_conv_w[i], bf(a_w_out[i]), batch, seq)
        else:
            j = i - n_self
            first = i == n_self
            outs = _proj(xs, g_mix[i], bf(b_w_q[j]), q_scale, y=y,
                         g_kv=g_kv if first else None, w_kv=bf(w_kv) if first else None)
            outs = list(outs)
            if y is not None:
                xs = outs.pop(0)
                y = None
            q = outs.pop(0)
            if first:
                k, v = outs
            o = _attention(q, k, v, batch, seq)
            w_o = bf(b_w_o[j])
        assert y is None
        if i % 2 == 0:
            xs = _ffn(xs, g_ffn[i], ffn_w_gu, ffn_w_down, i // 2, o=o, w_o=w_o)
        else:
            m = i // 2
            xs, h_tm, idx, w = _router(xs, g_ffn[i], moe_w_router[m], o=o, w_o=w_o)
            y = _experts(h_tm, idx, w, moe_w_gu, moe_w_down, m, n)
    assert y is not None
    return _final_norm(xs, y, g_final).reshape(batch, seq, d)
```

```python
import functools
import math

import jax
import jax.numpy as jnp
from jax import lax
from jax.experimental import pallas as pl
from jax.experimental.pallas import tpu as pltpu

F32 = jnp.float32
BF16 = jnp.bfloat16

RMS_EPS = 1e-6
LOG2_E = math.log2(math.e)
N_HEADS = 16
N_EXPERTS = 8
TOP_K = 2

V7X_LANES = 128
V7X_SUBLANES = 8
V7X_MXU_WIDTH = 256
V7X_VMEM_BYTES = 64 * 1024 * 1024
VMEM_LIMIT = V7X_VMEM_BYTES - 8 * 1024 * 1024

MIXER_ROWS = 1024
FFN_ROWS = 512
PROJ_ROWS = 1024
ROUTER_ROWS = 1024
EXPERT_ROWS = 512
ATTN_BLOCK = 256
ATTN_TILES = 4
ATTN_UNDERFLOW_LOG2 = -150.0
HEADS_PER_STEP = 2
CONV_CARRY_ROWS = V7X_SUBLANES
DMA_UNROLL = 64
EXPERT_FF_TILES = 3


def _params(*semantics):
    return pltpu.CompilerParams(dimension_semantics=semantics, vmem_limit_bytes=VMEM_LIMIT)


def _dot(a, b):
    return jnp.dot(a, b, preferred_element_type=F32)


def _rms_unit(x):
    return x * lax.rsqrt(jnp.mean(x * x, axis=-1, keepdims=True) + RMS_EPS)


def _load_token_major(ref, rows):
    parts = [ref[pl.ds(j, rows, stride=V7X_SUBLANES), :] for j in range(V7X_SUBLANES)]
    return jnp.concatenate(parts, axis=1)


def _store_token_major(ref, val, rows):
    for j in range(V7X_SUBLANES):
        ref[pl.ds(j, rows, stride=V7X_SUBLANES), :] = val[:, j * V7X_LANES:(j + 1) * V7X_LANES]


def _mixer_kernel(x_ref, g_ref, win_ref, cw_ref, wout_ref, out_ref, carry_ref, *, rows, d):
    s = pl.program_id(1)

    @pl.when(s == 0)
    def _():
        carry_ref[...] = jnp.zeros_like(carry_ref)

    x = x_ref[...]
    h = (_rms_unit(x) * g_ref[...]).astype(BF16)
    proj = _dot(h, win_ref[...])
    u = proj[:, :d] * proj[:, d:2 * d]
    b_gate = proj[:, 2 * d:]
    prev = carry_ref[...]
    row = lax.broadcasted_iota(jnp.int32, (rows, d), 0)
    u1 = jnp.where(row == 0, prev[CONV_CARRY_ROWS - 1:CONV_CARRY_ROWS, :], pltpu.roll(u, 1, 0))
    u2 = pltpu.roll(u, 2, 0)
    u2 = jnp.where(row == 0, prev[CONV_CARRY_ROWS - 2:CONV_CARRY_ROWS - 1, :], u2)
    u2 = jnp.where(row == 1, prev[CONV_CARRY_ROWS - 1:CONV_CARRY_ROWS, :], u2)
    carry_ref[...] = u[rows - CONV_CARRY_ROWS:, :]
    cw = cw_ref[...]
    conv = cw[0:1, :] * u2 + cw[1:2, :] * u1 + cw[2:3, :] * u
    mixed = (b_gate * conv).astype(BF16)
    out_ref[...] = x + _dot(mixed, wout_ref[...])


def _mixer(x, g, w_in, conv_w, w_out, batch, seq):
    n, d = x.shape
    rows = min(MIXER_ROWS, seq)
    tiles = seq // rows
    return pl.pallas_call(
        functools.partial(_mixer_kernel, rows=rows, d=d),
        grid=(batch, tiles),
        in_specs=[
            pl.BlockSpec((rows, d), lambda i, j: (i * tiles + j, 0)),
            pl.BlockSpec((1, d), lambda i, j: (0, 0)),
            pl.BlockSpec((d, 3 * d), lambda i, j: (0, 0)),
            pl.BlockSpec((3, d), lambda i, j: (0, 0)),
            pl.BlockSpec((d, d), lambda i, j: (0, 0)),
        ],
        out_specs=pl.BlockSpec((rows, d), lambda i, j: (i * tiles + j, 0)),
        out_shape=jax.ShapeDtypeStruct((n, d), F32),
        scratch_shapes=[pltpu.VMEM((CONV_CARRY_ROWS, d), F32)],
        compiler_params=_params("arbitrary", "arbitrary"),
        name="conv_mixer",
    )(x, g.reshape(1, d), w_in, conv_w, w_out)


def _ffn_kernel(*refs, has_o, d_ff):
    if has_o:
        x_ref, o_ref, wo_ref, g_ref, wgu_ref, wd_ref, out_ref = refs
    else:
        x_ref, g_ref, wgu_ref, wd_ref, out_ref = refs
    x = x_ref[...]
    if has_o:
        x = x + _dot(o_ref[...], wo_ref[...])
    h = (_rms_unit(x) * g_ref[...]).astype(BF16)
    acc = x
    for c0 in range(0, d_ff, V7X_MXU_WIDTH):
        c1 = c0 + V7X_MXU_WIDTH
        gate = _dot(h, wgu_ref[:, c0:c1])
        up = _dot(h, wgu_ref[:, d_ff + c0:d_ff + c1])
        act = (gate * jax.nn.sigmoid(gate) * up).astype(BF16)
        acc = acc + _dot(act, wd_ref[c0:c1, :])
    out_ref[...] = acc


def _ffn(x, g, w_gu, w_down, layer, o=None, w_o=None):
    n, d = x.shape
    d_ff = w_down.shape[1]
    assert d_ff % V7X_MXU_WIDTH == 0
    rows = min(FFN_ROWS, n)
    has_o = o is not None
    row_spec = pl.BlockSpec((rows, d), lambda i: (i, 0))
    once = pl.Buffered(1)
    in_specs = [row_spec]
    args = [x]
    if has_o:
        in_specs += [row_spec, pl.BlockSpec((d, d), lambda i: (0, 0), pipeline_mode=once)]
        args += [o, w_o]
    in_specs += [
        pl.BlockSpec((1, d), lambda i: (0, 0)),
        pl.BlockSpec((None, d, 2 * d_ff), lambda i: (layer, 0, 0), pipeline_mode=once),
        pl.BlockSpec((None, d_ff, d), lambda i: (layer, 0, 0), pipeline_mode=once),
    ]
    args += [g.reshape(1, d), w_gu, w_down]
    return pl.pallas_call(
        functools.partial(_ffn_kernel, has_o=has_o, d_ff=d_ff),
        grid=(n // rows,),
        in_specs=in_specs,
        out_specs=row_spec,
        out_shape=jax.ShapeDtypeStruct((n, d), F32),
        compiler_params=_params("arbitrary"),
        name="dense_swiglu",
    )(*args)


def _proj_kernel(*refs, has_y, has_kv, rows, d, q_scale):
    refs = list(refs)
    x_ref = refs.pop(0)
    if has_y:
        y0_ref = refs.pop(0)
        y1_ref = refs.pop(0)
    gq_ref = refs.pop(0)
    wq_ref = refs.pop(0)
    if has_kv:
        gkv_ref = refs.pop(0)
        wkv_ref = refs.pop(0)
    if has_y:
        xs_ref = refs.pop(0)
    q_ref = refs.pop(0)
    if has_kv:
        k_ref = refs.pop(0)
        v_ref = refs.pop(0)

    x = x_ref[...]
    if has_y:
        x = x + _load_token_major(y0_ref, rows) + _load_token_major(y1_ref, rows)
        xs_ref[...] = x
    xn = _rms_unit(x)
    q = _dot((xn * gq_ref[...]).astype(BF16), wq_ref[...])
    q_ref[...] = (q * q_scale).astype(BF16)
    if has_kv:
        kv = _dot((xn * gkv_ref[...]).astype(BF16), wkv_ref[...])
        k_ref[...] = kv[:, :d].astype(BF16)
        v_ref[...] = kv[:, d:].astype(BF16)


def _proj(x, g_q, w_q, q_scale, y=None, g_kv=None, w_kv=None):
    n, d = x.shape
    rows = min(PROJ_ROWS, n)
    has_y = y is not None
    has_kv = w_kv is not None
    row_spec = pl.BlockSpec((rows, d), lambda i: (i, 0))
    vec_spec = pl.BlockSpec((1, d), lambda i: (0, 0))
    in_specs = [row_spec]
    args = [x]
    if has_y:
        blocks_per_slot = n // rows
        tm_spec0 = pl.BlockSpec((rows * V7X_SUBLANES, V7X_LANES), lambda i: (i, 0))
        tm_spec1 = pl.BlockSpec((rows * V7X_SUBLANES, V7X_LANES),
                                lambda i: (i + blocks_per_slot, 0))
        in_specs += [tm_spec0, tm_spec1]
        args += [y, y]
    in_specs += [vec_spec, pl.BlockSpec((d, d), lambda i: (0, 0))]
    args += [g_q.reshape(1, d), w_q]
    if has_kv:
        in_specs += [vec_spec, pl.BlockSpec((d, 2 * d), lambda i: (0, 0))]
        args += [g_kv.reshape(1, d), w_kv]
    out_specs = []
    out_shape = []
    if has_y:
        out_specs.append(row_spec)
        out_shape.append(jax.ShapeDtypeStruct((n, d), F32))
    n_bf16 = 3 if has_kv else 1
    out_specs += [row_spec] * n_bf16
    out_shape += [jax.ShapeDtypeStruct((n, d), BF16)] * n_bf16
    return pl.pallas_call(
        functools.partial(_proj_kernel, has_y=has_y, has_kv=has_kv, rows=rows, d=d,
                          q_scale=q_scale),
        grid=(n // rows,),
        in_specs=in_specs,
        out_specs=out_specs,
        out_shape=out_shape,
        compiler_params=_params("arbitrary"),
        name="norm_qkv_proj",
    )(*args)


def _attn_kernel(q_ref, k_ref, v_ref, later_ref, o_ref, qs_ref, acc_ref, run_ref, *,
                 blk, dh, q_blocks):
    lane = lax.broadcasted_iota(jnp.int32, (1, HEADS_PER_STEP * dh), 1)
    head_mask = [(lane >= hh * dh) & (lane < (hh + 1) * dh) for hh in range(HEADS_PER_STEP)]
    stacked = HEADS_PER_STEP * blk

    def visit(work):
        later = later_ref[...]
        runs = {}
        totals = {}
        for slot, kb, masked in work:
            run = runs[slot] if slot in runs else run_ref[slot]
            start = pl.multiple_of(kb * blk, blk)
            k_blk = k_ref[pl.ds(start, blk), :]
            v_blk = v_ref[pl.ds(start, blk), :]
            z2 = lax.dot_general(qs_ref[slot], k_blk, (((1,), (1,)), ((), ())),
                                 preferred_element_type=F32)
            soft = jnp.log2(1.0 + jnp.exp2(-jnp.abs(z2)))
            log_beta = jnp.minimum(z2, 0.0) - soft
            log_1m_beta = log_beta - z2
            if masked:
                t_idx = lax.broadcasted_iota(jnp.int32, (stacked, blk), 0) & (blk - 1)
                s_idx = lax.broadcasted_iota(jnp.int32, (stacked, blk), 1)
                causal = s_idx < t_idx
                log_1m_beta = jnp.where(causal, log_1m_beta, 0.0)
            suffix = _dot(log_1m_beta.astype(BF16), later)
            attn = jnp.exp2(log_beta + suffix + run)
            if masked:
                attn = jnp.where(causal, attn, 0.0)
            av = _dot(attn.astype(BF16), v_blk)
            out = jnp.where(head_mask[0], av[:blk], av[blk:])
            totals[slot] = out if slot not in totals else totals[slot] + out
            runs[slot] = run + jnp.sum(log_1m_beta, axis=-1, keepdims=True)
        for slot in totals:
            acc_ref[slot] += totals[slot]
            run_ref[slot] = runs[slot]

    def exhausted(slot):
        return jnp.max(run_ref[slot]) <= ATTN_UNDERFLOW_LOG2

    def load_tile(slot, qi):
        q = q_ref[pl.ds(pl.multiple_of(qi * blk, blk), blk), :]
        for hh in range(HEADS_PER_STEP):
            qs_ref[slot, hh * blk:(hh + 1) * blk, :] = jnp.where(head_mask[hh], q, jnp.zeros_like(q))
        acc_ref[slot] = jnp.zeros_like(acc_ref[slot])
        run_ref[slot] = jnp.zeros_like(run_ref[slot])

    def finish_tile(slot, qi, done):
        def pair(it, done):
            kb = qi - 2 - 2 * it

            @pl.when(jnp.logical_not(done))
            def _():
                visit([(slot, kb, False), (slot, kb - 1, False)])

            return done | exhausted(slot)

        done = lax.fori_loop(0, jnp.maximum(qi - 1, 0) // 2, pair, done)

        @pl.when(jnp.logical_not(done) & (qi >= 2) & (qi % 2 == 0))
        def _():
            visit([(slot, 0, False)])

        o_ref[pl.ds(pl.multiple_of(qi * blk, blk), blk), :] = acc_ref[slot].astype(BF16)

    def tile_group(m, carry):
        tiles = [ATTN_TILES * m + t for t in range(ATTN_TILES)]
        for t, qi in enumerate(tiles):
            load_tile(t, qi)
        diagonals = [(t, qi, True) for t, qi in enumerate(tiles)]
        befores = [(t, qi - 1, False) for t, qi in enumerate(tiles)]

        @pl.when(m == 0)
        def _():
            visit(diagonals + befores[1:])

        @pl.when(m > 0)
        def _():
            visit(diagonals + befores)

        done = [exhausted(t) for t in range(ATTN_TILES)]
        for t, qi in enumerate(tiles):
            finish_tile(t, qi, done[t])
        return carry

    lax.fori_loop(0, q_blocks // ATTN_TILES, tile_group, 0)


def _attention(q, k, v, batch, seq):
    n, d = q.shape
    dh = d // N_HEADS
    blk = min(ATTN_BLOCK, seq)
    width = HEADS_PER_STEP * dh
    q_blocks = seq // blk
    assert q_blocks % ATTN_TILES == 0
    idx = jnp.arange(blk, dtype=jnp.int32)
    later = (idx[:, None] > idx[None, :]).astype(BF16)
    col_blocks = d // width
    seq_spec = pl.BlockSpec((seq, width), lambda g: (g // col_blocks, g % col_blocks))
    return pl.pallas_call(
        functools.partial(_attn_kernel, blk=blk, dh=dh, q_blocks=q_blocks),
        grid=(batch * col_blocks,),
        in_specs=[seq_spec, seq_spec, seq_spec, pl.BlockSpec((blk, blk), lambda g: (0, 0))],
        out_specs=seq_spec,
        out_shape=jax.ShapeDtypeStruct((n, d), BF16),
        scratch_shapes=[pltpu.VMEM((ATTN_TILES, HEADS_PER_STEP * blk, width), BF16),
                        pltpu.VMEM((ATTN_TILES, blk, width), F32),
                        pltpu.VMEM((ATTN_TILES, HEADS_PER_STEP * blk, 1), F32)],
        compiler_params=_params("arbitrary"),
        name="stick_breaking_attention",
    )(q, k, v, later)


def _router_kernel(*refs, has_o, rows):
    refs = list(refs)
    x_ref = refs.pop(0)
    if has_o:
        o_ref = refs.pop(0)
        wo_ref = refs.pop(0)
    g_ref = refs.pop(0)
    wr_ref = refs.pop(0)
    if has_o:
        xs_ref = refs.pop(0)
    h_ref, idx_ref, w_ref = refs

    x = x_ref[...]
    if has_o:
        x = x + _dot(o_ref[...], wo_ref[...])
        xs_ref[...] = x
    h = _rms_unit(x) * g_ref[...]
    _store_token_major(h_ref, h, rows)

    h_hi = h.astype(BF16)
    h_lo = (h - h_hi.astype(F32)).astype(BF16)
    both = _dot(h_hi, wr_ref[...])
    logits = both[:, :V7X_LANES] + (both[:, V7X_LANES:] + _dot(h_lo, wr_ref[:, :V7X_LANES]))

    lane = lax.broadcasted_iota(jnp.int32, logits.shape, 1)
    lane_f = lane.astype(F32)
    neg_inf = jnp.float32(-jnp.inf)
    no_lane = jnp.float32(V7X_LANES)
    lg = jnp.where(lane < N_EXPERTS, logits, neg_inf)
    m1 = jnp.max(lg, axis=-1, keepdims=True)
    i1 = jnp.min(jnp.where(lg == m1, lane_f, no_lane), axis=-1, keepdims=True)
    lg2 = jnp.where(lane_f == i1, neg_inf, lg)
    m2 = jnp.max(lg2, axis=-1, keepdims=True)
    i2 = jnp.min(jnp.where(lg2 == m2, lane_f, no_lane), axis=-1, keepdims=True)
    e2 = jnp.exp(m2 - m1)
    w1 = 1.0 / (1.0 + e2)
    w2 = e2 * w1
    idx_ref[...] = jnp.where(lane == 0, i1, jnp.where(lane == 1, i2, 0.0)).astype(jnp.int32)
    w_ref[...] = jnp.where(lane == 0, w1, jnp.where(lane == 1, w2, 0.0))


def _router(x, g, w_router, o=None, w_o=None):
    n, d = x.shape
    rows = min(ROUTER_ROWS, n)
    has_o = o is not None
    wr = jnp.zeros((d, V7X_LANES), F32).at[:, :N_EXPERTS].set(w_router)
    wr_hi = wr.astype(BF16)
    wr_lo = (wr - wr_hi.astype(F32)).astype(BF16)
    wr_both = jnp.concatenate([wr_hi, wr_lo], axis=1)
    row_spec = pl.BlockSpec((rows, d), lambda i: (i, 0))
    lane_spec = pl.BlockSpec((rows, V7X_LANES), lambda i: (i, 0))
    in_specs = [row_spec]
    args = [x]
    if has_o:
        in_specs += [row_spec, pl.BlockSpec((d, d), lambda i: (0, 0))]
        args += [o, w_o]
    in_specs += [pl.BlockSpec((1, d), lambda i: (0, 0)),
                 pl.BlockSpec((d, 2 * V7X_LANES), lambda i: (0, 0))]
    args += [g.reshape(1, d), wr_both]
    out_specs = []
    out_shape = []
    if has_o:
        out_specs.append(row_spec)
        out_shape.append(jax.ShapeDtypeStruct((n, d), F32))
    out_specs += [pl.BlockSpec((rows * V7X_SUBLANES, V7X_LANES), lambda i: (i, 0)),
                  lane_spec, lane_spec]
    out_shape += [jax.ShapeDtypeStruct((n * V7X_SUBLANES, V7X_LANES), F32),
                  jax.ShapeDtypeStruct((n, V7X_LANES), jnp.int32),
                  jax.ShapeDtypeStruct((n, V7X_LANES), F32)]
    outs = pl.pallas_call(
        functools.partial(_router_kernel, has_o=has_o, rows=rows),
        grid=(n // rows,),
        in_specs=in_specs,
        out_specs=out_specs,
        out_shape=out_shape,
        compiler_params=_params("arbitrary"),
        name="router_top2",
    )(*args)
    if has_o:
        xs, h_tm, idx, w = outs
    else:
        h_tm, idx, w = outs
        xs = x
    return xs, h_tm, idx[:, :TOP_K], w[:, :TOP_K]


def _plan_routing(idx, w, rows):
    n = idx.shape[0]
    n_pairs = n * TOP_K
    n_tiles = n_pairs // rows + N_EXPERTS
    e_flat = idx.reshape(n_pairs)
    pair_id = jnp.arange(n_pairs, dtype=jnp.int32)
    _, order, w_sorted = lax.sort((e_flat, pair_id, w.reshape(n_pairs)), num_keys=1, is_stable=True)
    experts = jnp.arange(N_EXPERTS, dtype=jnp.int32)
    counts = jnp.sum((e_flat[:, None] == experts[None, :]).astype(jnp.int32), axis=0)
    starts = jnp.cumsum(counts) - counts
    pad_counts = ((counts + rows - 1) // rows) * rows
    pad_ends = jnp.cumsum(pad_counts)
    pad_starts = pad_ends - pad_counts
    tile_start = jnp.arange(n_tiles, dtype=jnp.int32) * rows
    e_of_tile = jnp.sum((tile_start[:, None] >= pad_ends[None, :]).astype(jnp.int32), axis=1)
    tile_expert = jnp.minimum(e_of_tile, N_EXPERTS - 1)
    tile_within = tile_start - pad_starts[tile_expert]
    tile_count = jnp.where(e_of_tile < N_EXPERTS,
                           jnp.clip(counts[tile_expert] - tile_within, 0, rows), 0)
    lane = jnp.arange(rows, dtype=jnp.int32)
    sorted_pos = jnp.clip((starts[tile_expert] + tile_within)[:, None] + lane[None, :],
                          0, n_pairs - 1)
    valid = lane[None, :] < tile_count[:, None]
    pair = order[sorted_pos]
    token = pair // TOP_K
    slot = pair % TOP_K
    src = jnp.where(valid, token, 0)
    dst = jnp.where(valid, slot * n + token, 0)
    gate = jnp.where(valid, w_sorted[sorted_pos], 0.0)
    return (src.reshape(n_tiles, 1, rows), dst.reshape(n_tiles, 1, rows),
            gate.reshape(n_tiles * rows, 1), tile_expert, tile_count)


def _expert_kernel(te_ref, tc_ref, src_ref, src_next_ref, dst_ref, gate_ref, h_hbm,
                   wg_ref, wu_ref, wd_ref, y_hbm,
                   gbuf_ref, ybuf_ref, hbf_ref, acc_ref, gsem, ssem, *,
                   rows, n_f, n_tiles, overlap):
    i = pl.program_id(0)
    j = pl.program_id(1)
    count = tc_ref[i]
    valid = count > 0
    full = count == rows
    first_step = j == 0
    last_step = j == n_f - 1
    has_next = (i + 1 < n_tiles) & (tc_ref[jnp.minimum(i + 1, n_tiles - 1)] > 0)
    prev_in_flight = (i > 0) & (tc_ref[jnp.maximum(i - 1, 0)] == rows)
    tile_rows = rows * V7X_SUBLANES

    def row_window(r):
        return pl.ds(pl.multiple_of(r * V7X_SUBLANES, V7X_SUBLANES), V7X_SUBLANES)

    def start_gather(idx_ref, r, slot):
        pltpu.make_async_copy(h_hbm.at[row_window(idx_ref[0, r]), :],
                              gbuf_ref.at[slot, row_window(r), :], gsem.at[slot]).start()

    def wait_gather(slot):
        pltpu.make_async_copy(h_hbm.at[pl.ds(0, tile_rows), :], gbuf_ref.at[slot],
                              gsem.at[slot]).wait()

    def scatter_copy(r):
        return pltpu.make_async_copy(ybuf_ref.at[row_window(r), :],
                                     y_hbm.at[row_window(dst_ref[0, r]), :], ssem.at[0])

    def wait_full_scatter():
        pltpu.make_async_copy(ybuf_ref, y_hbm.at[pl.ds(0, tile_rows), :], ssem.at[0]).wait()

    def for_all_rows(fn):
        def body(b, carry):
            for u in range(DMA_UNROLL):
                fn(b * DMA_UNROLL + u)
            return carry
        lax.fori_loop(0, rows // DMA_UNROLL, body, 0)

    def for_rows(n_rows, fn):
        def body(r, carry):
            fn(r)
            return carry
        lax.fori_loop(0, n_rows, body, 0)

    @pl.when(first_step & (i == 0))
    def _():
        for_all_rows(lambda r: start_gather(src_ref, r, 0))

    for slot in range(2):
        mine = valid & first_step & (i % 2 == slot)

        @pl.when(mine)
        def _(slot=slot):
            wait_gather(slot)
            hbf_ref[...] = _load_token_major(gbuf_ref.at[slot], rows).astype(BF16)
            acc_ref[...] = jnp.zeros_like(acc_ref)

        @pl.when(mine & has_next)
        def _(slot=slot):
            for_all_rows(lambda r: start_gather(src_next_ref, r, 1 - slot))

    def matmul_step(skip):
        h = hbf_ref[...]
        part = None
        width = wg_ref.shape[3]
        step = V7X_MXU_WIDTH * EXPERT_FF_TILES
        for c0 in range(skip, width, step):
            c1 = min(c0 + step, width)
            gate = _dot(h, wg_ref[0, 0, :, c0:c1].astype(BF16))
            up = _dot(h, wu_ref[0, 0, :, c0:c1].astype(BF16))
            act = (gate * jax.nn.sigmoid(gate) * up).astype(BF16)
            contrib = _dot(act, wd_ref[0, 0, c0:c1, :].astype(BF16))
            part = contrib if part is None else part + contrib
        acc_ref[...] += part

    upper = jnp.where(i % 2 == 0, j, n_f - 1 - j) == 1

    @pl.when(valid & jnp.logical_not(upper))
    def _():
        matmul_step(0)

    @pl.when(valid & upper)
    def _():
        matmul_step(overlap)

    @pl.when(valid & last_step & prev_in_flight)
    def _():
        wait_full_scatter()

    @pl.when(valid & last_step)
    def _():
        _store_token_major(ybuf_ref, acc_ref[...] * gate_ref[...], rows)

    @pl.when(valid & last_step & full)
    def _():
        for_all_rows(lambda r: scatter_copy(r).start())

    @pl.when(valid & last_step & full & jnp.logical_not(has_next))
    def _():
        wait_full_scatter()

    @pl.when(valid & last_step & jnp.logical_not(full))
    def _():
        for_rows(count, lambda r: scatter_copy(r).start())
        for_rows(count, lambda r: scatter_copy(r).wait())


def _experts(h_tm, idx, w, w_gu, w_down, layer, n):
    d = w_down.shape[3]
    d_ff = w_down.shape[2]
    rows = min(EXPERT_ROWS, n)
    assert rows % DMA_UNROLL == 0
    n_f = 2
    assert d_ff % V7X_MXU_WIDTH == 0 and d_ff // V7X_MXU_WIDTH >= n_f
    tf = -(-(d_ff // V7X_MXU_WIDTH) // n_f) * V7X_MXU_WIDTH
    overlap = n_f * tf - d_ff
    src, dst, gate, tile_expert, tile_count = _plan_routing(idx, w, rows)
    n_tiles = src.shape[0]

    def f_off(i, j, tc):
        pos = jnp.where(tc[i] > 0, jnp.where(i % 2 == 0, j, n_f - 1 - j), 0)
        return pl.multiple_of(pos * (d_ff - tf), V7X_MXU_WIDTH)

    def window(rows_cols, offsets):
        return pl.BlockSpec((pl.Element(1), pl.Element(1)) + tuple(pl.Element(s) for s in rows_cols),
                            offsets)

    idx_spec = functools.partial(pl.BlockSpec, (None, 1, rows), memory_space=pltpu.SMEM)
    grid_spec = pltpu.PrefetchScalarGridSpec(
        num_scalar_prefetch=2,
        grid=(n_tiles, n_f),
        in_specs=[
            idx_spec(lambda i, j, te, tc: (i, 0, 0)),
            idx_spec(lambda i, j, te, tc: (jnp.minimum(i + 1, n_tiles - 1), 0, 0)),
            idx_spec(lambda i, j, te, tc: (i, 0, 0)),
            pl.BlockSpec((rows, 1), lambda i, j, te, tc: (i, 0)),
            pl.BlockSpec(memory_space=pl.ANY),
            window((d, tf), lambda i, j, te, tc: (layer, te[i], 0, f_off(i, j, tc))),
            window((d, tf), lambda i, j, te, tc: (
                layer, te[i], 0, pl.multiple_of(d_ff + f_off(i, j, tc), V7X_MXU_WIDTH))),
            window((tf, d), lambda i, j, te, tc: (layer, te[i], f_off(i, j, tc), 0)),
        ],
        out_specs=pl.BlockSpec(memory_space=pl.ANY),
        scratch_shapes=[
            pltpu.VMEM((2, rows * V7X_SUBLANES, V7X_LANES), F32),
            pltpu.VMEM((rows * V7X_SUBLANES, V7X_LANES), F32),
            pltpu.VMEM((rows, d), BF16),
            pltpu.VMEM((rows, d), F32),
            pltpu.SemaphoreType.DMA((2,)),
            pltpu.SemaphoreType.DMA((1,)),
        ],
    )
    return pl.pallas_call(
        functools.partial(_expert_kernel, rows=rows, n_f=n_f, n_tiles=n_tiles, overlap=overlap),
        grid_spec=grid_spec,
        out_shape=jax.ShapeDtypeStruct((n * TOP_K * V7X_SUBLANES, V7X_LANES), F32),
        compiler_params=_params("arbitrary", "arbitrary"),
        name="routed_swiglu",
    )(tile_expert, tile_count, src, src, dst, gate, h_tm, w_gu, w_gu, w_down)


def _final_kernel(x_ref, y0_ref, y1_ref, g_ref, out_ref, *, rows):
    x = x_ref[...] + _load_token_major(y0_ref, rows) + _load_token_major(y1_ref, rows)
    out_ref[...] = _rms_unit(x) * g_ref[...]


def _final_norm(x, y, g):
    n, d = x.shape
    rows = min(PROJ_ROWS, n)
    blocks_per_slot = n // rows
    row_spec = pl.BlockSpec((rows, d), lambda i: (i, 0))
    return pl.pallas_call(
        functools.partial(_final_kernel, rows=rows),
        grid=(n // rows,),
        in_specs=[
            row_spec,
            pl.BlockSpec((rows * V7X_SUBLANES, V7X_LANES), lambda i: (i, 0)),
            pl.BlockSpec((rows * V7X_SUBLANES, V7X_LANES), lambda i: (i + blocks_per_slot, 0)),
            pl.BlockSpec((1, d), lambda i: (0, 0)),
        ],
        out_specs=row_spec,
        out_shape=jax.ShapeDtypeStruct((n, d), F32),
        compiler_params=_params("arbitrary"),
        name="final_norm",
    )(x, y, y, g.reshape(1, d))


def kernel(x, g_mix, g_ffn, g_final, a_w_in, a_conv_w, a_w_out, g_kv, w_kv, b_w_q, b_w_o,
           ffn_w_gu, ffn_w_down, moe_w_router, moe_w_gu, moe_w_down):
    batch, seq, d = x.shape
    n = batch * seq
    depth = g_mix.shape[0]
    n_self = a_w_in.shape[0]
    q_scale = LOG2_E / math.sqrt(d // N_HEADS)

    def bf(a):
        return a.astype(BF16)

    ffn_w_gu, ffn_w_down = bf(ffn_w_gu), bf(ffn_w_down)

    xs = x.reshape(n, d)
    y = None
    k = v = None
    for i in range(depth):
        o = w_o = None
        if i < n_self:
            assert y is None
            xs = _mixer(xs, g_mix[i], bf(a_w_in[i]), a_conv_w[i], bf(a_w_out[i]), batch, seq)
        else:
            j = i - n_self
            first = i == n_self
            outs = _proj(xs, g_mix[i], bf(b_w_q[j]), q_scale, y=y,
                         g_kv=g_kv if first else None, w_kv=bf(w_kv) if first else None)
            outs = list(outs)
            if y is not None:
                xs = outs.pop(0)
                y = None
            q = outs.pop(0)
            if first:
                k, v = outs
            o = _attention(q, k, v, batch, seq)
            w_o = bf(b_w_o[j])
        assert y is None
        if i % 2 == 0:
            xs = _ffn(xs, g_ffn[i], ffn_w_gu, ffn_w_down, i // 2, o=o, w_o=w_o)
        else:
            m = i // 2
            xs, h_tm, idx, w = _router(xs, g_ffn[i], moe_w_router[m], o=o, w_o=w_o)
            y = _experts(h_tm, idx, w, moe_w_gu, moe_w_down, m, n)
    assert y is not None
    return _final_norm(xs, y, g_final).reshape(batch, seq, d)
```

```python
import functools
import math

import jax
import jax.numpy as jnp
from jax import lax
from jax.experimental import pallas as pl
from jax.experimental.pallas import tpu as pltpu

F32 = jnp.float32
BF16 = jnp.bfloat16

RMS_EPS = 1e-6
LOG2_E = math.log2(math.e)
N_HEADS = 16
N_EXPERTS = 8
TOP_K = 2

V7X_LANES = 128
V7X_SUBLANES = 8
V7X_MXU_WIDTH = 256
V7X_VMEM_BYTES = 64 * 1024 * 1024
VMEM_LIMIT = V7X_VMEM_BYTES - 8 * 1024 * 1024

MIXER_ROWS = 1024
FFN_ROWS = 512
PROJ_ROWS = 1024
ROUTER_ROWS = 1024
EXPERT_ROWS = 768
ATTN_BLOCK = 256
ATTN_TILES = 4
ATTN_UNDERFLOW_LOG2 = -150.0
HEADS_PER_STEP = 2
CONV_CARRY_ROWS = V7X_SUBLANES
DMA_UNROLL = 64
EXPERT_FF_TILES = 3


def _params(*semantics):
    return pltpu.CompilerParams(dimension_semantics=semantics, vmem_limit_bytes=VMEM_LIMIT)


def _dot(a, b):
    return jnp.dot(a, b, preferred_element_type=F32)


def _rms_unit(x):
    return x * lax.rsqrt(jnp.mean(x * x, axis=-1, keepdims=True) + RMS_EPS)


def _load_token_major(ref, rows):
    parts = [ref[pl.ds(j, rows, stride=V7X_SUBLANES), :] for j in range(V7X_SUBLANES)]
    return jnp.concatenate(parts, axis=1)


def _store_token_major(ref, val, rows):
    for j in range(V7X_SUBLANES):
        ref[pl.ds(j, rows, stride=V7X_SUBLANES), :] = val[:, j * V7X_LANES:(j + 1) * V7X_LANES]


def _mixer_kernel(x_ref, g_ref, win_ref, cw_ref, wout_ref, out_ref, carry_ref, *, rows, d):
    s = pl.program_id(1)

    @pl.when(s == 0)
    def _():
        carry_ref[...] = jnp.zeros_like(carry_ref)

    x = x_ref[...]
    h = (_rms_unit(x) * g_ref[...]).astype(BF16)
    proj = _dot(h, win_ref[...])
    u = proj[:, :d] * proj[:, d:2 * d]
    b_gate = proj[:, 2 * d:]
    prev = carry_ref[...]
    row = lax.broadcasted_iota(jnp.int32, (rows, d), 0)
    u1 = jnp.where(row == 0, prev[CONV_CARRY_ROWS - 1:CONV_CARRY_ROWS, :], pltpu.roll(u, 1, 0))
    u2 = pltpu.roll(u, 2, 0)
    u2 = jnp.where(row == 0, prev[CONV_CARRY_ROWS - 2:CONV_CARRY_ROWS - 1, :], u2)
    u2 = jnp.where(row == 1, prev[CONV_CARRY_ROWS - 1:CONV_CARRY_ROWS, :], u2)
    carry_ref[...] = u[rows - CONV_CARRY_ROWS:, :]
    cw = cw_ref[...]
    conv = cw[0:1, :] * u2 + cw[1:2, :] * u1 + cw[2:3, :] * u
    mixed = (b_gate * conv).astype(BF16)
    out_ref[...] = x + _dot(mixed, wout_ref[...])


def _mixer(x, g, w_in, conv_w, w_out, batch, seq):
    n, d = x.shape
    rows = min(MIXER_ROWS, seq)
    tiles = seq // rows
    return pl.pallas_call(
        functools.partial(_mixer_kernel, rows=rows, d=d),
        grid=(batch, tiles),
        in_specs=[
            pl.BlockSpec((rows, d), lambda i, j: (i * tiles + j, 0)),
            pl.BlockSpec((1, d), lambda i, j: (0, 0)),
            pl.BlockSpec((d, 3 * d), lambda i, j: (0, 0)),
            pl.BlockSpec((3, d), lambda i, j: (0, 0)),
            pl.BlockSpec((d, d), lambda i, j: (0, 0)),
        ],
        out_specs=pl.BlockSpec((rows, d), lambda i, j: (i * tiles + j, 0)),
        out_shape=jax.ShapeDtypeStruct((n, d), F32),
        scratch_shapes=[pltpu.VMEM((CONV_CARRY_ROWS, d), F32)],
        compiler_params=_params("arbitrary", "arbitrary"),
        name="conv_mixer",
    )(x, g.reshape(1, d), w_in, conv_w, w_out)


def _ffn_kernel(*refs, has_o, d_ff):
    if has_o:
        x_ref, o_ref, wo_ref, g_ref, wgu_ref, wd_ref, out_ref = refs
    else:
        x_ref, g_ref, wgu_ref, wd_ref, out_ref = refs
    x = x_ref[...]
    if has_o:
        x = x + _dot(o_ref[...], wo_ref[...])
    h = (_rms_unit(x) * g_ref[...]).astype(BF16)
    acc = x
    for c0 in range(0, d_ff, V7X_MXU_WIDTH):
        c1 = c0 + V7X_MXU_WIDTH
        gate = _dot(h, wgu_ref[:, c0:c1])
        up = _dot(h, wgu_ref[:, d_ff + c0:d_ff + c1])
        act = (gate * jax.nn.sigmoid(gate) * up).astype(BF16)
        acc = acc + _dot(act, wd_ref[c0:c1, :])
    out_ref[...] = acc


def _ffn(x, g, w_gu, w_down, layer, o=None, w_o=None):
    n, d = x.shape
    d_ff = w_down.shape[1]
    assert d_ff % V7X_MXU_WIDTH == 0
    rows = min(FFN_ROWS, n)
    has_o = o is not None
    row_spec = pl.BlockSpec((rows, d), lambda i: (i, 0))
    once = pl.Buffered(1)
    in_specs = [row_spec]
    args = [x]
    if has_o:
        in_specs += [row_spec, pl.BlockSpec((d, d), lambda i: (0, 0), pipeline_mode=once)]
        args += [o, w_o]
    in_specs += [
        pl.BlockSpec((1, d), lambda i: (0, 0)),
        pl.BlockSpec((None, d, 2 * d_ff), lambda i: (layer, 0, 0), pipeline_mode=once),
        pl.BlockSpec((None, d_ff, d), lambda i: (layer, 0, 0), pipeline_mode=once),
    ]
    args += [g.reshape(1, d), w_gu, w_down]
    return pl.pallas_call(
        functools.partial(_ffn_kernel, has_o=has_o, d_ff=d_ff),
        grid=(n // rows,),
        in_specs=in_specs,
        out_specs=row_spec,
        out_shape=jax.ShapeDtypeStruct((n, d), F32),
        compiler_params=_params("arbitrary"),
        name="dense_swiglu",
    )(*args)


def _proj_kernel(*refs, has_y, has_kv, rows, d, q_scale):
    refs = list(refs)
    x_ref = refs.pop(0)
    if has_y:
        y0_ref = refs.pop(0)
        y1_ref = refs.pop(0)
    gq_ref = refs.pop(0)
    wq_ref = refs.pop(0)
    if has_kv:
        gkv_ref = refs.pop(0)
        wkv_ref = refs.pop(0)
    if has_y:
        xs_ref = refs.pop(0)
    q_ref = refs.pop(0)
    if has_kv:
        k_ref = refs.pop(0)
        v_ref = refs.pop(0)

    x = x_ref[...]
    if has_y:
        x = x + _load_token_major(y0_ref, rows) + _load_token_major(y1_ref, rows)
        xs_ref[...] = x
    xn = _rms_unit(x)
    q = _dot((xn * gq_ref[...]).astype(BF16), wq_ref[...])
    q_ref[...] = (q * q_scale).astype(BF16)
    if has_kv:
        kv = _dot((xn * gkv_ref[...]).astype(BF16), wkv_ref[...])
        k_ref[...] = kv[:, :d].astype(BF16)
        v_ref[...] = kv[:, d:].astype(BF16)


def _proj(x, g_q, w_q, q_scale, y=None, g_kv=None, w_kv=None):
    n, d = x.shape
    rows = min(PROJ_ROWS, n)
    has_y = y is not None
    has_kv = w_kv is not None
    row_spec = pl.BlockSpec((rows, d), lambda i: (i, 0))
    vec_spec = pl.BlockSpec((1, d), lambda i: (0, 0))
    in_specs = [row_spec]
    args = [x]
    if has_y:
        blocks_per_slot = n // rows
        tm_spec0 = pl.BlockSpec((rows * V7X_SUBLANES, V7X_LANES), lambda i: (i, 0))
        tm_spec1 = pl.BlockSpec((rows * V7X_SUBLANES, V7X_LANES),
                                lambda i: (i + blocks_per_slot, 0))
        in_specs += [tm_spec0, tm_spec1]
        args += [y, y]
    in_specs += [vec_spec, pl.BlockSpec((d, d), lambda i: (0, 0))]
    args += [g_q.reshape(1, d), w_q]
    if has_kv:
        in_specs += [vec_spec, pl.BlockSpec((d, 2 * d), lambda i: (0, 0))]
        args += [g_kv.reshape(1, d), w_kv]
    out_specs = []
    out_shape = []
    if has_y:
        out_specs.append(row_spec)
        out_shape.append(jax.ShapeDtypeStruct((n, d), F32))
    n_bf16 = 3 if has_kv else 1
    out_specs += [row_spec] * n_bf16
    out_shape += [jax.ShapeDtypeStruct((n, d), BF16)] * n_bf16
    return pl.pallas_call(
        functools.partial(_proj_kernel, has_y=has_y, has_kv=has_kv, rows=rows, d=d,
                          q_scale=q_scale),
        grid=(n // rows,),
        in_specs=in_specs,
        out_specs=out_specs,
        out_shape=out_shape,
        compiler_params=_params("arbitrary"),
        name="norm_qkv_proj",
    )(*args)


def _attn_kernel(q_ref, k_ref, v_ref, later_ref, o_ref, qs_ref, acc_ref, run_ref, *,
                 blk, dh, q_blocks):
    lane = lax.broadcasted_iota(jnp.int32, (1, HEADS_PER_STEP * dh), 1)
    head_mask = [(lane >= hh * dh) & (lane < (hh + 1) * dh) for hh in range(HEADS_PER_STEP)]
    stacked = HEADS_PER_STEP * blk

    def visit(work):
        later = later_ref[...]
        runs = {}
        totals = {}
        for slot, kb, masked in work:
            run = runs[slot] if slot in runs else run_ref[slot]
            start = pl.multiple_of(kb * blk, blk)
            k_blk = k_ref[pl.ds(start, blk), :]
            v_blk = v_ref[pl.ds(start, blk), :]
            z2 = lax.dot_general(qs_ref[slot], k_blk, (((1,), (1,)), ((), ())),
                                 preferred_element_type=F32)
            soft = jnp.log2(1.0 + jnp.exp2(-jnp.abs(z2)))
            log_beta = jnp.minimum(z2, 0.0) - soft
            log_1m_beta = log_beta - z2
            if masked:
                t_idx = lax.broadcasted_iota(jnp.int32, (stacked, blk), 0) & (blk - 1)
                s_idx = lax.broadcasted_iota(jnp.int32, (stacked, blk), 1)
                causal = s_idx < t_idx
                log_1m_beta = jnp.where(causal, log_1m_beta, 0.0)
            suffix = _dot(log_1m_beta.astype(BF16), later)
            attn = jnp.exp2(log_beta + suffix + run)
            if masked:
                attn = jnp.where(causal, attn, 0.0)
            av = _dot(attn.astype(BF16), v_blk)
            out = jnp.where(head_mask[0], av[:blk], av[blk:])
            totals[slot] = out if slot not in totals else totals[slot] + out
            runs[slot] = run + jnp.sum(log_1m_beta, axis=-1, keepdims=True)
        for slot in totals:
            acc_ref[slot] += totals[slot]
            run_ref[slot] = runs[slot]

    def exhausted(slot):
        return jnp.max(run_ref[slot]) <= ATTN_UNDERFLOW_LOG2

    def load_tile(slot, qi):
        q = q_ref[pl.ds(pl.multiple_of(qi * blk, blk), blk), :]
        for hh in range(HEADS_PER_STEP):
            qs_ref[slot, hh * blk:(hh + 1) * blk, :] = jnp.where(head_mask[hh], q, jnp.zeros_like(q))
        acc_ref[slot] = jnp.zeros_like(acc_ref[slot])
        run_ref[slot] = jnp.zeros_like(run_ref[slot])

    def finish_tile(slot, qi, done):
        def pair(it, done):
            kb = qi - 2 - 2 * it

            @pl.when(jnp.logical_not(done))
            def _():
                visit([(slot, kb, False), (slot, kb - 1, False)])

            return done | exhausted(slot)

        done = lax.fori_loop(0, jnp.maximum(qi - 1, 0) // 2, pair, done)

        @pl.when(jnp.logical_not(done) & (qi >= 2) & (qi % 2 == 0))
        def _():
            visit([(slot, 0, False)])

        o_ref[pl.ds(pl.multiple_of(qi * blk, blk), blk), :] = acc_ref[slot].astype(BF16)

    def tile_group(m, carry):
        tiles = [ATTN_TILES * m + t for t in range(ATTN_TILES)]
        for t, qi in enumerate(tiles):
            load_tile(t, qi)
        diagonals = [(t, qi, True) for t, qi in enumerate(tiles)]
        befores = [(t, qi - 1, False) for t, qi in enumerate(tiles)]

        @pl.when(m == 0)
        def _():
            visit(diagonals + befores[1:])

        @pl.when(m > 0)
        def _():
            visit(diagonals + befores)

        done = [exhausted(t) for t in range(ATTN_TILES)]
        for t, qi in enumerate(tiles):
            finish_tile(t, qi, done[t])
        return carry

    lax.fori_loop(0, q_blocks // ATTN_TILES, tile_group, 0)


def _attention(q, k, v, batch, seq):
    n, d = q.shape
    dh = d // N_HEADS
    blk = min(ATTN_BLOCK, seq)
    width = HEADS_PER_STEP * dh
    q_blocks = seq // blk
    assert q_blocks % ATTN_TILES == 0
    idx = jnp.arange(blk, dtype=jnp.int32)
    later = (idx[:, None] > idx[None, :]).astype(BF16)
    col_blocks = d // width
    seq_spec = pl.BlockSpec((seq, width), lambda g: (g // col_blocks, g % col_blocks))
    return pl.pallas_call(
        functools.partial(_attn_kernel, blk=blk, dh=dh, q_blocks=q_blocks),
        grid=(batch * col_blocks,),
        in_specs=[seq_spec, seq_spec, seq_spec, pl.BlockSpec((blk, blk), lambda g: (0, 0))],
        out_specs=seq_spec,
        out_shape=jax.ShapeDtypeStruct((n, d), BF16),
        scratch_shapes=[pltpu.VMEM((ATTN_TILES, HEADS_PER_STEP * blk, width), BF16),
                        pltpu.VMEM((ATTN_TILES, blk, width), F32),
                        pltpu.VMEM((ATTN_TILES, HEADS_PER_STEP * blk, 1), F32)],
        compiler_params=_params("arbitrary"),
        name="stick_breaking_attention",
    )(q, k, v, later)


def _router_kernel(*refs, has_o, rows):
    refs = list(refs)
    x_ref = refs.pop(0)
    if has_o:
        o_ref = refs.pop(0)
        wo_ref = refs.pop(0)
    g_ref = refs.pop(0)
    wr_ref = refs.pop(0)
    if has_o:
        xs_ref = refs.pop(0)
    h_ref, idx_ref, w_ref = refs

    x = x_ref[...]
    if has_o:
        x = x + _dot(o_ref[...], wo_ref[...])
        xs_ref[...] = x
    h = _rms_unit(x) * g_ref[...]
    _store_token_major(h_ref, h, rows)

    h_hi = h.astype(BF16)
    h_lo = (h - h_hi.astype(F32)).astype(BF16)
    both = _dot(h_hi, wr_ref[...])
    logits = both[:, :V7X_LANES] + (both[:, V7X_LANES:] + _dot(h_lo, wr_ref[:, :V7X_LANES]))

    lane = lax.broadcasted_iota(jnp.int32, logits.shape, 1)
    lane_f = lane.astype(F32)
    neg_inf = jnp.float32(-jnp.inf)
    no_lane = jnp.float32(V7X_LANES)
    lg = jnp.where(lane < N_EXPERTS, logits, neg_inf)
    m1 = jnp.max(lg, axis=-1, keepdims=True)
    i1 = jnp.min(jnp.where(lg == m1, lane_f, no_lane), axis=-1, keepdims=True)
    lg2 = jnp.where(lane_f == i1, neg_inf, lg)
    m2 = jnp.max(lg2, axis=-1, keepdims=True)
    i2 = jnp.min(jnp.where(lg2 == m2, lane_f, no_lane), axis=-1, keepdims=True)
    e2 = jnp.exp(m2 - m1)
    w1 = 1.0 / (1.0 + e2)
    w2 = e2 * w1
    idx_ref[...] = jnp.where(lane == 0, i1, jnp.where(lane == 1, i2, 0.0)).astype(jnp.int32)
    w_ref[...] = jnp.where(lane == 0, w1, jnp.where(lane == 1, w2, 0.0))


def _router(x, g, w_router, o=None, w_o=None):
    n, d = x.shape
    rows = min(ROUTER_ROWS, n)
    has_o = o is not None
    wr = jnp.zeros((d, V7X_LANES), F32).at[:, :N_EXPERTS].set(w_router)
    wr_hi = wr.astype(BF16)
    wr_lo = (wr - wr_hi.astype(F32)).astype(BF16)
    wr_both = jnp.concatenate([wr_hi, wr_lo], axis=1)
    row_spec = pl.BlockSpec((rows, d), lambda i: (i, 0))
    lane_spec = pl.BlockSpec((rows, V7X_LANES), lambda i: (i, 0))
    in_specs = [row_spec]
    args = [x]
    if has_o:
        in_specs += [row_spec, pl.BlockSpec((d, d), lambda i: (0, 0))]
        args += [o, w_o]
    in_specs += [pl.BlockSpec((1, d), lambda i: (0, 0)),
                 pl.BlockSpec((d, 2 * V7X_LANES), lambda i: (0, 0))]
    args += [g.reshape(1, d), wr_both]
    out_specs = []
    out_shape = []
    if has_o:
        out_specs.append(row_spec)
        out_shape.append(jax.ShapeDtypeStruct((n, d), F32))
    out_specs += [pl.BlockSpec((rows * V7X_SUBLANES, V7X_LANES), lambda i: (i, 0)),
                  lane_spec, lane_spec]
    out_shape += [jax.ShapeDtypeStruct((n * V7X_SUBLANES, V7X_LANES), F32),
                  jax.ShapeDtypeStruct((n, V7X_LANES), jnp.int32),
                  jax.ShapeDtypeStruct((n, V7X_LANES), F32)]
    outs = pl.pallas_call(
        functools.partial(_router_kernel, has_o=has_o, rows=rows),
        grid=(n // rows,),
        in_specs=in_specs,
        out_specs=out_specs,
        out_shape=out_shape,
        compiler_params=_params("arbitrary"),
        name="router_top2",
    )(*args)
    if has_o:
        xs, h_tm, idx, w = outs
    else:
        h_tm, idx, w = outs
        xs = x
    return xs, h_tm, idx[:, :TOP_K], w[:, :TOP_K]


def _plan_routing(idx, w, rows):
    n = idx.shape[0]
    n_pairs = n * TOP_K
    n_tiles = -(-n_pairs // rows) + N_EXPERTS
    e_flat = idx.reshape(n_pairs)
    pair_id = jnp.arange(n_pairs, dtype=jnp.int32)
    _, order, w_sorted = lax.sort((e_flat, pair_id, w.reshape(n_pairs)), num_keys=1, is_stable=True)
    experts = jnp.arange(N_EXPERTS, dtype=jnp.int32)
    counts = jnp.sum((e_flat[:, None] == experts[None, :]).astype(jnp.int32), axis=0)
    starts = jnp.cumsum(counts) - counts
    pad_counts = ((counts + rows - 1) // rows) * rows
    pad_ends = jnp.cumsum(pad_counts)
    pad_starts = pad_ends - pad_counts
    tile_start = jnp.arange(n_tiles, dtype=jnp.int32) * rows
    e_of_tile = jnp.sum((tile_start[:, None] >= pad_ends[None, :]).astype(jnp.int32), axis=1)
    tile_expert = jnp.minimum(e_of_tile, N_EXPERTS - 1)
    tile_within = tile_start - pad_starts[tile_expert]
    tile_count = jnp.where(e_of_tile < N_EXPERTS,
                           jnp.clip(counts[tile_expert] - tile_within, 0, rows), 0)
    lane = jnp.arange(rows, dtype=jnp.int32)
    sorted_pos = jnp.clip((starts[tile_expert] + tile_within)[:, None] + lane[None, :],
                          0, n_pairs - 1)
    valid = lane[None, :] < tile_count[:, None]
    pair = order[sorted_pos]
    token = pair // TOP_K
    slot = pair % TOP_K
    src = jnp.where(valid, token, 0)
    dst = jnp.where(valid, slot * n + token, 0)
    gate = jnp.where(valid, w_sorted[sorted_pos], 0.0)
    return (src.reshape(n_tiles, 1, rows), dst.reshape(n_tiles, 1, rows),
            gate.reshape(n_tiles * rows, 1), tile_expert, tile_count)


def _expert_kernel(te_ref, tc_ref, src_ref, src_next_ref, dst_ref, gate_ref, h_hbm,
                   wg_ref, wu_ref, wd_ref, y_hbm,
                   gbuf_ref, ybuf_ref, hbf_ref, acc_ref, gsem, ssem, *,
                   rows, n_f, n_tiles, overlap):
    i = pl.program_id(0)
    j = pl.program_id(1)
    count = tc_ref[i]
    valid = count > 0
    full = count == rows
    first_step = j == 0
    last_step = j == n_f - 1
    has_next = (i + 1 < n_tiles) & (tc_ref[jnp.minimum(i + 1, n_tiles - 1)] > 0)
    prev_in_flight = (i > 0) & (tc_ref[jnp.maximum(i - 1, 0)] == rows)
    tile_rows = rows * V7X_SUBLANES

    def row_window(r):
        return pl.ds(pl.multiple_of(r * V7X_SUBLANES, V7X_SUBLANES), V7X_SUBLANES)

    def start_gather(idx_ref, r, slot):
        pltpu.make_async_copy(h_hbm.at[row_window(idx_ref[0, r]), :],
                              gbuf_ref.at[slot, row_window(r), :], gsem.at[slot]).start()

    def wait_gather(slot):
        pltpu.make_async_copy(h_hbm.at[pl.ds(0, tile_rows), :], gbuf_ref.at[slot],
                              gsem.at[slot]).wait()

    def scatter_copy(r):
        return pltpu.make_async_copy(ybuf_ref.at[row_window(r), :],
                                     y_hbm.at[row_window(dst_ref[0, r]), :], ssem.at[0])

    def wait_full_scatter():
        pltpu.make_async_copy(ybuf_ref, y_hbm.at[pl.ds(0, tile_rows), :], ssem.at[0]).wait()

    def for_all_rows(fn):
        def body(b, carry):
            for u in range(DMA_UNROLL):
                fn(b * DMA_UNROLL + u)
            return carry
        lax.fori_loop(0, rows // DMA_UNROLL, body, 0)

    def for_rows(n_rows, fn):
        def body(r, carry):
            fn(r)
            return carry
        lax.fori_loop(0, n_rows, body, 0)

    @pl.when(first_step & (i == 0))
    def _():
        for_all_rows(lambda r: start_gather(src_ref, r, 0))

    for slot in range(2):
        mine = valid & first_step & (i % 2 == slot)

        @pl.when(mine)
        def _(slot=slot):
            wait_gather(slot)
            hbf_ref[...] = _load_token_major(gbuf_ref.at[slot], rows).astype(BF16)
            acc_ref[...] = jnp.zeros_like(acc_ref)

        @pl.when(mine & has_next)
        def _(slot=slot):
            for_all_rows(lambda r: start_gather(src_next_ref, r, 1 - slot))

    def matmul_step(skip):
        h = hbf_ref[...]
        part = None
        width = wg_ref.shape[3]
        step = V7X_MXU_WIDTH * EXPERT_FF_TILES
        for c0 in range(skip, width, step):
            c1 = min(c0 + step, width)
            gate = _dot(h, wg_ref[0, 0, :, c0:c1].astype(BF16))
            up = _dot(h, wu_ref[0, 0, :, c0:c1].astype(BF16))
            act = (gate * jax.nn.sigmoid(gate) * up).astype(BF16)
            contrib = _dot(act, wd_ref[0, 0, c0:c1, :].astype(BF16))
            part = contrib if part is None else part + contrib
        acc_ref[...] += part

    upper = jnp.where(i % 2 == 0, j, n_f - 1 - j) == 1

    @pl.when(valid & jnp.logical_not(upper))
    def _():
        matmul_step(0)

    @pl.when(valid & upper)
    def _():
        matmul_step(overlap)

    @pl.when(valid & last_step & prev_in_flight)
    def _():
        wait_full_scatter()

    @pl.when(valid & last_step)
    def _():
        _store_token_major(ybuf_ref, acc_ref[...] * gate_ref[...], rows)

    @pl.when(valid & last_step & full)
    def _():
        for_all_rows(lambda r: scatter_copy(r).start())

    @pl.when(valid & last_step & full & jnp.logical_not(has_next))
    def _():
        wait_full_scatter()

    @pl.when(valid & last_step & jnp.logical_not(full))
    def _():
        for_rows(count, lambda r: scatter_copy(r).start())
        for_rows(count, lambda r: scatter_copy(r).wait())


def _experts(h_tm, idx, w, w_gu, w_down, layer, n):
    d = w_down.shape[3]
    d_ff = w_down.shape[2]
    rows = min(EXPERT_ROWS, n)
    assert rows % DMA_UNROLL == 0
    n_f = 2
    assert d_ff % V7X_MXU_WIDTH == 0 and d_ff // V7X_MXU_WIDTH >= n_f
    tf = -(-(d_ff // V7X_MXU_WIDTH) // n_f) * V7X_MXU_WIDTH
    overlap = n_f * tf - d_ff
    src, dst, gate, tile_expert, tile_count = _plan_routing(idx, w, rows)
    n_tiles = src.shape[0]

    def f_off(i, j, tc):
        pos = jnp.where(tc[i] > 0, jnp.where(i % 2 == 0, j, n_f - 1 - j), 0)
        return pl.multiple_of(pos * (d_ff - tf), V7X_MXU_WIDTH)

    def window(rows_cols, offsets):
        return pl.BlockSpec((pl.Element(1), pl.Element(1)) + tuple(pl.Element(s) for s in rows_cols),
                            offsets)

    idx_spec = functools.partial(pl.BlockSpec, (None, 1, rows), memory_space=pltpu.SMEM)
    grid_spec = pltpu.PrefetchScalarGridSpec(
        num_scalar_prefetch=2,
        grid=(n_tiles, n_f),
        in_specs=[
            idx_spec(lambda i, j, te, tc: (i, 0, 0)),
            idx_spec(lambda i, j, te, tc: (jnp.minimum(i + 1, n_tiles - 1), 0, 0)),
            idx_spec(lambda i, j, te, tc: (i, 0, 0)),
            pl.BlockSpec((rows, 1), lambda i, j, te, tc: (i, 0)),
            pl.BlockSpec(memory_space=pl.ANY),
            window((d, tf), lambda i, j, te, tc: (layer, te[i], 0, f_off(i, j, tc))),
            window((d, tf), lambda i, j, te, tc: (
                layer, te[i], 0, pl.multiple_of(d_ff + f_off(i, j, tc), V7X_MXU_WIDTH))),
            window((tf, d), lambda i, j, te, tc: (layer, te[i], f_off(i, j, tc), 0)),
        ],
        out_specs=pl.BlockSpec(memory_space=pl.ANY),
        scratch_shapes=[
            pltpu.VMEM((2, rows * V7X_SUBLANES, V7X_LANES), F32),
            pltpu.VMEM((rows * V7X_SUBLANES, V7X_LANES), F32),
            pltpu.VMEM((rows, d), BF16),
            pltpu.VMEM((rows, d), F32),
            pltpu.SemaphoreType.DMA((2,)),
            pltpu.SemaphoreType.DMA((1,)),
        ],
    )
    return pl.pallas_call(
        functools.partial(_expert_kernel, rows=rows, n_f=n_f, n_tiles=n_tiles, overlap=overlap),
        grid_spec=grid_spec,
        out_shape=jax.ShapeDtypeStruct((n * TOP_K * V7X_SUBLANES, V7X_LANES), F32),
        compiler_params=_params("arbitrary", "arbitrary"),
        name="routed_swiglu",
    )(tile_expert, tile_count, src, src, dst, gate, h_tm, w_gu, w_gu, w_down)


def _final_kernel(x_ref, y0_ref, y1_ref, g_ref, out_ref, *, rows):
    x = x_ref[...] + _load_token_major(y0_ref, rows) + _load_token_major(y1_ref, rows)
    out_ref[...] = _rms_unit(x) * g_ref[...]


def _final_norm(x, y, g):
    n, d = x.shape
    rows = min(PROJ_ROWS, n)
    blocks_per_slot = n // rows
    row_spec = pl.BlockSpec((rows, d), lambda i: (i, 0))
    return pl.pallas_call(
        functools.partial(_final_kernel, rows=rows),
        grid=(n // rows,),
        in_specs=[
            row_spec,
            pl.BlockSpec((rows * V7X_SUBLANES, V7X_LANES), lambda i: (i, 0)),
            pl.BlockSpec((rows * V7X_SUBLANES, V7X_LANES), lambda i: (i + blocks_per_slot, 0)),
            pl.BlockSpec((1, d), lambda i: (0, 0)),
        ],
        out_specs=row_spec,
        out_shape=jax.ShapeDtypeStruct((n, d), F32),
        compiler_params=_params("arbitrary"),
        name="final_norm",
    )(x, y, y, g.reshape(1, d))


def kernel(x, g_mix, g_ffn, g_final, a_w_in, a_conv_w, a_w_out, g_kv, w_kv, b_w_q, b_w_o,
           ffn_w_gu, ffn_w_down, moe_w_router, moe_w_gu, moe_w_down):
    batch, seq, d = x.shape
    n = batch * seq
    depth = g_mix.shape[0]
    n_self = a_w_in.shape[0]
    q_scale = LOG2_E / math.sqrt(d // N_HEADS)

    def bf(a):
        return a.astype(BF16)

    ffn_w_gu, ffn_w_down = bf(ffn_w_gu), bf(ffn_w_down)

    xs = x.reshape(n, d)
    y = None
    k = v = None
    for i in range(depth):
        o = w_o = None
        if i < n_self:
            assert y is None
            xs = _mixer(xs, g_mix[i], bf(a_w_in[i]), a_conv_w[i], bf(a_w_out[i]), batch, seq)
        else:
            j = i - n_self
            first = i == n_self
            outs = _proj(xs, g_mix[i], bf(b_w_q[j]), q_scale, y=y,
                         g_kv=g_kv if first else None, w_kv=bf(w_kv) if first else None)
            outs = list(outs)
            if y is not None:
                xs = outs.pop(0)
                y = None
            q = outs.pop(0)
            if first:
                k, v = outs
            o = _attention(q, k, v, batch, seq)
            w_o = bf(b_w_o[j])
        assert y is None
        if i % 2 == 0:
            xs = _ffn(xs, g_ffn[i], ffn_w_gu, ffn_w_down, i // 2, o=o, w_o=w_o)
        else:
            m = i // 2
            xs, h_tm, idx, w = _router(xs, g_ffn[i], moe_w_router[m], o=o, w_o=w_o)
            y = _experts(h_tm, idx, w, moe_w_gu, moe_w_down, m, n)
    assert y is not None
    return _final_norm(xs, y, g_final).reshape(batch, seq, d)
```

```python
import functools
import math

import jax
import jax.numpy as jnp
from jax import lax
from jax.experimental import pallas as pl
from jax.experimental.pallas import tpu as pltpu

F32 = jnp.float32
BF16 = jnp.bfloat16

RMS_EPS = 1e-6
LOG2_E = math.log2(math.e)
N_HEADS = 16
N_EXPERTS = 8
TOP_K = 2

V7X_LANES = 128
V7X_SUBLANES = 8
V7X_MXU_WIDTH = 256
V7X_VMEM_BYTES = 64 * 1024 * 1024
VMEM_LIMIT = V7X_VMEM_BYTES - 8 * 1024 * 1024

MIXER_ROWS = 1024
FFN_ROWS = 512
PROJ_ROWS = 1024
ROUTER_ROWS = 1024
EXPERT_ROWS = 512
ATTN_BLOCK = 256
ATTN_TILES = 4
ATTN_UNDERFLOW_LOG2 = -150.0
HEADS_PER_STEP = 2
CONV_CARRY_ROWS = V7X_SUBLANES
DMA_UNROLL = 64
ROW_COPY_PRIORITY = 1
EXPERT_FF_TILES = 3


def _params(*semantics):
    return pltpu.CompilerParams(dimension_semantics=semantics, vmem_limit_bytes=VMEM_LIMIT)


def _dot(a, b):
    return jnp.dot(a, b, preferred_element_type=F32)


def _rms_unit(x):
    return x * lax.rsqrt(jnp.mean(x * x, axis=-1, keepdims=True) + RMS_EPS)


def _load_token_major(ref, rows):
    parts = [ref[pl.ds(j, rows, stride=V7X_SUBLANES), :] for j in range(V7X_SUBLANES)]
    return jnp.concatenate(parts, axis=1)


def _store_token_major(ref, val, rows):
    for j in range(V7X_SUBLANES):
        ref[pl.ds(j, rows, stride=V7X_SUBLANES), :] = val[:, j * V7X_LANES:(j + 1) * V7X_LANES]


def _mixer_kernel(x_ref, g_ref, win_ref, cw_ref, wout_ref, out_ref, carry_ref, *, rows, d):
    s = pl.program_id(1)

    @pl.when(s == 0)
    def _():
        carry_ref[...] = jnp.zeros_like(carry_ref)

    x = x_ref[...]
    h = (_rms_unit(x) * g_ref[...]).astype(BF16)
    proj = _dot(h, win_ref[...])
    u = proj[:, :d] * proj[:, d:2 * d]
    b_gate = proj[:, 2 * d:]
    prev = carry_ref[...]
    row = lax.broadcasted_iota(jnp.int32, (rows, d), 0)
    u1 = jnp.where(row == 0, prev[CONV_CARRY_ROWS - 1:CONV_CARRY_ROWS, :], pltpu.roll(u, 1, 0))
    u2 = pltpu.roll(u, 2, 0)
    u2 = jnp.where(row == 0, prev[CONV_CARRY_ROWS - 2:CONV_CARRY_ROWS - 1, :], u2)
    u2 = jnp.where(row == 1, prev[CONV_CARRY_ROWS - 1:CONV_CARRY_ROWS, :], u2)
    carry_ref[...] = u[rows - CONV_CARRY_ROWS:, :]
    cw = cw_ref[...]
    conv = cw[0:1, :] * u2 + cw[1:2, :] * u1 + cw[2:3, :] * u
    mixed = (b_gate * conv).astype(BF16)
    out_ref[...] = x + _dot(mixed, wout_ref[...])


def _mixer(x, g, w_in, conv_w, w_out, batch, seq):
    n, d = x.shape
    rows = min(MIXER_ROWS, seq)
    tiles = seq // rows
    return pl.pallas_call(
        functools.partial(_mixer_kernel, rows=rows, d=d),
        grid=(batch, tiles),
        in_specs=[
            pl.BlockSpec((rows, d), lambda i, j: (i * tiles + j, 0)),
            pl.BlockSpec((1, d), lambda i, j: (0, 0)),
            pl.BlockSpec((d, 3 * d), lambda i, j: (0, 0)),
            pl.BlockSpec((3, d), lambda i, j: (0, 0)),
            pl.BlockSpec((d, d), lambda i, j: (0, 0)),
        ],
        out_specs=pl.BlockSpec((rows, d), lambda i, j: (i * tiles + j, 0)),
        out_shape=jax.ShapeDtypeStruct((n, d), F32),
        scratch_shapes=[pltpu.VMEM((CONV_CARRY_ROWS, d), F32)],
        compiler_params=_params("arbitrary", "arbitrary"),
        name="conv_mixer",
    )(x, g.reshape(1, d), w_in, conv_w, w_out)


def _ffn_kernel(*refs, has_o, d_ff):
    if has_o:
        x_ref, o_ref, wo_ref, g_ref, wgu_ref, wd_ref, out_ref = refs
    else:
        x_ref, g_ref, wgu_ref, wd_ref, out_ref = refs
    x = x_ref[...]
    if has_o:
        x = x + _dot(o_ref[...], wo_ref[...])
    h = (_rms_unit(x) * g_ref[...]).astype(BF16)
    acc = x
    for c0 in range(0, d_ff, V7X_MXU_WIDTH):
        c1 = c0 + V7X_MXU_WIDTH
        gate = _dot(h, wgu_ref[:, c0:c1])
        up = _dot(h, wgu_ref[:, d_ff + c0:d_ff + c1])
        act = (gate * jax.nn.sigmoid(gate) * up).astype(BF16)
        acc = acc + _dot(act, wd_ref[c0:c1, :])
    out_ref[...] = acc


def _ffn(x, g, w_gu, w_down, layer, o=None, w_o=None):
    n, d = x.shape
    d_ff = w_down.shape[1]
    assert d_ff % V7X_MXU_WIDTH == 0
    rows = min(FFN_ROWS, n)
    has_o = o is not None
    row_spec = pl.BlockSpec((rows, d), lambda i: (i, 0))
    once = pl.Buffered(1)
    in_specs = [row_spec]
    args = [x]
    if has_o:
        in_specs += [row_spec, pl.BlockSpec((d, d), lambda i: (0, 0), pipeline_mode=once)]
        args += [o, w_o]
    in_specs += [
        pl.BlockSpec((1, d), lambda i: (0, 0)),
        pl.BlockSpec((None, d, 2 * d_ff), lambda i: (layer, 0, 0), pipeline_mode=once),
        pl.BlockSpec((None, d_ff, d), lambda i: (layer, 0, 0), pipeline_mode=once),
    ]
    args += [g.reshape(1, d), w_gu, w_down]
    return pl.pallas_call(
        functools.partial(_ffn_kernel, has_o=has_o, d_ff=d_ff),
        grid=(n // rows,),
        in_specs=in_specs,
        out_specs=row_spec,
        out_shape=jax.ShapeDtypeStruct((n, d), F32),
        compiler_params=_params("arbitrary"),
        name="dense_swiglu",
    )(*args)


def _proj_kernel(*refs, has_y, has_kv, rows, d, q_scale):
    refs = list(refs)
    x_ref = refs.pop(0)
    if has_y:
        y0_ref = refs.pop(0)
        y1_ref = refs.pop(0)
    gq_ref = refs.pop(0)
    wq_ref = refs.pop(0)
    if has_kv:
        gkv_ref = refs.pop(0)
        wkv_ref = refs.pop(0)
    if has_y:
        xs_ref = refs.pop(0)
    q_ref = refs.pop(0)
    if has_kv:
        k_ref = refs.pop(0)
        v_ref = refs.pop(0)

    x = x_ref[...]
    if has_y:
        x = x + _load_token_major(y0_ref, rows) + _load_token_major(y1_ref, rows)
        xs_ref[...] = x
    xn = _rms_unit(x)
    q = _dot((xn * gq_ref[...]).astype(BF16), wq_ref[...])
    q_ref[...] = (q * q_scale).astype(BF16)
    if has_kv:
        kv = _dot((xn * gkv_ref[...]).astype(BF16), wkv_ref[...])
        k_ref[...] = kv[:, :d].astype(BF16)
        v_ref[...] = kv[:, d:].astype(BF16)


def _proj(x, g_q, w_q, q_scale, y=None, g_kv=None, w_kv=None):
    n, d = x.shape
    rows = min(PROJ_ROWS, n)
    has_y = y is not None
    has_kv = w_kv is not None
    row_spec = pl.BlockSpec((rows, d), lambda i: (i, 0))
    vec_spec = pl.BlockSpec((1, d), lambda i: (0, 0))
    in_specs = [row_spec]
    args = [x]
    if has_y:
        blocks_per_slot = n // rows
        tm_spec0 = pl.BlockSpec((rows * V7X_SUBLANES, V7X_LANES), lambda i: (i, 0))
        tm_spec1 = pl.BlockSpec((rows * V7X_SUBLANES, V7X_LANES),
                                lambda i: (i + blocks_per_slot, 0))
        in_specs += [tm_spec0, tm_spec1]
        args += [y, y]
    in_specs += [vec_spec, pl.BlockSpec((d, d), lambda i: (0, 0))]
    args += [g_q.reshape(1, d), w_q]
    if has_kv:
        in_specs += [vec_spec, pl.BlockSpec((d, 2 * d), lambda i: (0, 0))]
        args += [g_kv.reshape(1, d), w_kv]
    out_specs = []
    out_shape = []
    if has_y:
        out_specs.append(row_spec)
        out_shape.append(jax.ShapeDtypeStruct((n, d), F32))
    n_bf16 = 3 if has_kv else 1
    out_specs += [row_spec] * n_bf16
    out_shape += [jax.ShapeDtypeStruct((n, d), BF16)] * n_bf16
    return pl.pallas_call(
        functools.partial(_proj_kernel, has_y=has_y, has_kv=has_kv, rows=rows, d=d,
                          q_scale=q_scale),
        grid=(n // rows,),
        in_specs=in_specs,
        out_specs=out_specs,
        out_shape=out_shape,
        compiler_params=_params("arbitrary"),
        name="norm_qkv_proj",
    )(*args)


def _attn_kernel(q_ref, k_ref, v_ref, later_ref, o_ref, qs_ref, acc_ref, run_ref, *,
                 blk, dh, q_blocks):
    lane = lax.broadcasted_iota(jnp.int32, (1, HEADS_PER_STEP * dh), 1)
    head_mask = [(lane >= hh * dh) & (lane < (hh + 1) * dh) for hh in range(HEADS_PER_STEP)]
    stacked = HEADS_PER_STEP * blk

    def visit(work):
        later = later_ref[...]
        runs = {}
        totals = {}
        for slot, kb, masked in work:
            run = runs[slot] if slot in runs else run_ref[slot]
            start = pl.multiple_of(kb * blk, blk)
            k_blk = k_ref[pl.ds(start, blk), :]
            v_blk = v_ref[pl.ds(start, blk), :]
            z2 = lax.dot_general(qs_ref[slot], k_blk, (((1,), (1,)), ((), ())),
                                 preferred_element_type=F32)
            soft = jnp.log2(1.0 + jnp.exp2(-jnp.abs(z2)))
            log_beta = jnp.minimum(z2, 0.0) - soft
            log_1m_beta = log_beta - z2
            if masked:
                t_idx = lax.broadcasted_iota(jnp.int32, (stacked, blk), 0) & (blk - 1)
                s_idx = lax.broadcasted_iota(jnp.int32, (stacked, blk), 1)
                causal = s_idx < t_idx
                log_1m_beta = jnp.where(causal, log_1m_beta, 0.0)
            suffix = _dot(log_1m_beta.astype(BF16), later)
            attn = jnp.exp2(log_beta + suffix + run)
            if masked:
                attn = jnp.where(causal, attn, 0.0)
            av = _dot(attn.astype(BF16), v_blk)
            out = jnp.where(head_mask[0], av[:blk], av[blk:])
            totals[slot] = out if slot not in totals else totals[slot] + out
            runs[slot] = run + jnp.sum(log_1m_beta, axis=-1, keepdims=True)
        for slot in totals:
            acc_ref[slot] += totals[slot]
            run_ref[slot] = runs[slot]

    def exhausted(slot):
        return jnp.max(run_ref[slot]) <= ATTN_UNDERFLOW_LOG2

    def load_tile(slot, qi):
        q = q_ref[pl.ds(pl.multiple_of(qi * blk, blk), blk), :]
        for hh in range(HEADS_PER_STEP):
            qs_ref[slot, hh * blk:(hh + 1) * blk, :] = jnp.where(head_mask[hh], q, jnp.zeros_like(q))
        acc_ref[slot] = jnp.zeros_like(acc_ref[slot])
        run_ref[slot] = jnp.zeros_like(run_ref[slot])

    def finish_tile(slot, qi, done):
        def pair(it, done):
            kb = qi - 2 - 2 * it

            @pl.when(jnp.logical_not(done))
            def _():
                visit([(slot, kb, False), (slot, kb - 1, False)])

            return done | exhausted(slot)

        done = lax.fori_loop(0, jnp.maximum(qi - 1, 0) // 2, pair, done)

        @pl.when(jnp.logical_not(done) & (qi >= 2) & (qi % 2 == 0))
        def _():
            visit([(slot, 0, False)])

        o_ref[pl.ds(pl.multiple_of(qi * blk, blk), blk), :] = acc_ref[slot].astype(BF16)

    def tile_group(m, carry):
        tiles = [ATTN_TILES * m + t for t in range(ATTN_TILES)]
        for t, qi in enumerate(tiles):
            load_tile(t, qi)
        diagonals = [(t, qi, True) for t, qi in enumerate(tiles)]
        befores = [(t, qi - 1, False) for t, qi in enumerate(tiles)]

        @pl.when(m == 0)
        def _():
            visit(diagonals + befores[1:])

        @pl.when(m > 0)
        def _():
            visit(diagonals + befores)

        done = [exhausted(t) for t in range(ATTN_TILES)]
        for t, qi in enumerate(tiles):
            finish_tile(t, qi, done[t])
        return carry

    lax.fori_loop(0, q_blocks // ATTN_TILES, tile_group, 0)


def _attention(q, k, v, batch, seq):
    n, d = q.shape
    dh = d // N_HEADS
    blk = min(ATTN_BLOCK, seq)
    width = HEADS_PER_STEP * dh
    q_blocks = seq // blk
    assert q_blocks % ATTN_TILES == 0
    idx = jnp.arange(blk, dtype=jnp.int32)
    later = (idx[:, None] > idx[None, :]).astype(BF16)
    col_blocks = d // width
    seq_spec = pl.BlockSpec((seq, width), lambda g: (g // col_blocks, g % col_blocks))
    return pl.pallas_call(
        functools.partial(_attn_kernel, blk=blk, dh=dh, q_blocks=q_blocks),
        grid=(batch * col_blocks,),
        in_specs=[seq_spec, seq_spec, seq_spec, pl.BlockSpec((blk, blk), lambda g: (0, 0))],
        out_specs=seq_spec,
        out_shape=jax.ShapeDtypeStruct((n, d), BF16),
        scratch_shapes=[pltpu.VMEM((ATTN_TILES, HEADS_PER_STEP * blk, width), BF16),
                        pltpu.VMEM((ATTN_TILES, blk, width), F32),
                        pltpu.VMEM((ATTN_TILES, HEADS_PER_STEP * blk, 1), F32)],
        compiler_params=_params("arbitrary"),
        name="stick_breaking_attention",
    )(q, k, v, later)


def _router_kernel(*refs, has_o, rows):
    refs = list(refs)
    x_ref = refs.pop(0)
    if has_o:
        o_ref = refs.pop(0)
        wo_ref = refs.pop(0)
    g_ref = refs.pop(0)
    wr_ref = refs.pop(0)
    if has_o:
        xs_ref = refs.pop(0)
    h_ref, idx_ref, w_ref = refs

    x = x_ref[...]
    if has_o:
        x = x + _dot(o_ref[...], wo_ref[...])
        xs_ref[...] = x
    h = _rms_unit(x) * g_ref[...]
    _store_token_major(h_ref, h, rows)

    h_hi = h.astype(BF16)
    h_lo = (h - h_hi.astype(F32)).astype(BF16)
    both = _dot(h_hi, wr_ref[...])
    logits = both[:, :V7X_LANES] + (both[:, V7X_LANES:] + _dot(h_lo, wr_ref[:, :V7X_LANES]))

    lane = lax.broadcasted_iota(jnp.int32, logits.shape, 1)
    lane_f = lane.astype(F32)
    neg_inf = jnp.float32(-jnp.inf)
    no_lane = jnp.float32(V7X_LANES)
    lg = jnp.where(lane < N_EXPERTS, logits, neg_inf)
    m1 = jnp.max(lg, axis=-1, keepdims=True)
    i1 = jnp.min(jnp.where(lg == m1, lane_f, no_lane), axis=-1, keepdims=True)
    lg2 = jnp.where(lane_f == i1, neg_inf, lg)
    m2 = jnp.max(lg2, axis=-1, keepdims=True)
    i2 = jnp.min(jnp.where(lg2 == m2, lane_f, no_lane), axis=-1, keepdims=True)
    e2 = jnp.exp(m2 - m1)
    w1 = 1.0 / (1.0 + e2)
    w2 = e2 * w1
    idx_ref[...] = jnp.where(lane == 0, i1, jnp.where(lane == 1, i2, 0.0)).astype(jnp.int32)
    w_ref[...] = jnp.where(lane == 0, w1, jnp.where(lane == 1, w2, 0.0))


def _router(x, g, w_router, o=None, w_o=None):
    n, d = x.shape
    rows = min(ROUTER_ROWS, n)
    has_o = o is not None
    wr = jnp.zeros((d, V7X_LANES), F32).at[:, :N_EXPERTS].set(w_router)
    wr_hi = wr.astype(BF16)
    wr_lo = (wr - wr_hi.astype(F32)).astype(BF16)
    wr_both = jnp.concatenate([wr_hi, wr_lo], axis=1)
    row_spec = pl.BlockSpec((rows, d), lambda i: (i, 0))
    lane_spec = pl.BlockSpec((rows, V7X_LANES), lambda i: (i, 0))
    in_specs = [row_spec]
    args = [x]
    if has_o:
        in_specs += [row_spec, pl.BlockSpec((d, d), lambda i: (0, 0))]
        args += [o, w_o]
    in_specs += [pl.BlockSpec((1, d), lambda i: (0, 0)),
                 pl.BlockSpec((d, 2 * V7X_LANES), lambda i: (0, 0))]
    args += [g.reshape(1, d), wr_both]
    out_specs = []
    out_shape = []
    if has_o:
        out_specs.append(row_spec)
        out_shape.append(jax.ShapeDtypeStruct((n, d), F32))
    out_specs += [pl.BlockSpec((rows * V7X_SUBLANES, V7X_LANES), lambda i: (i, 0)),
                  lane_spec, lane_spec]
    out_shape += [jax.ShapeDtypeStruct((n * V7X_SUBLANES, V7X_LANES), F32),
                  jax.ShapeDtypeStruct((n, V7X_LANES), jnp.int32),
                  jax.ShapeDtypeStruct((n, V7X_LANES), F32)]
    outs = pl.pallas_call(
        functools.partial(_router_kernel, has_o=has_o, rows=rows),
        grid=(n // rows,),
        in_specs=in_specs,
        out_specs=out_specs,
        out_shape=out_shape,
        compiler_params=_params("arbitrary"),
        name="router_top2",
    )(*args)
    if has_o:
        xs, h_tm, idx, w = outs
    else:
        h_tm, idx, w = outs
        xs = x
    return xs, h_tm, idx[:, :TOP_K], w[:, :TOP_K]


def _plan_routing(idx, w, rows):
    n = idx.shape[0]
    n_pairs = n * TOP_K
    n_tiles = n_pairs // rows + N_EXPERTS
    e_flat = idx.reshape(n_pairs)
    pair_id = jnp.arange(n_pairs, dtype=jnp.int32)
    _, order, w_sorted = lax.sort((e_flat, pair_id, w.reshape(n_pairs)), num_keys=1, is_stable=True)
    experts = jnp.arange(N_EXPERTS, dtype=jnp.int32)
    counts = jnp.sum((e_flat[:, None] == experts[None, :]).astype(jnp.int32), axis=0)
    starts = jnp.cumsum(counts) - counts
    pad_counts = ((counts + rows - 1) // rows) * rows
    pad_ends = jnp.cumsum(pad_counts)
    pad_starts = pad_ends - pad_counts
    tile_start = jnp.arange(n_tiles, dtype=jnp.int32) * rows
    e_of_tile = jnp.sum((tile_start[:, None] >= pad_ends[None, :]).astype(jnp.int32), axis=1)
    tile_expert = jnp.minimum(e_of_tile, N_EXPERTS - 1)
    tile_within = tile_start - pad_starts[tile_expert]
    tile_count = jnp.where(e_of_tile < N_EXPERTS,
                           jnp.clip(counts[tile_expert] - tile_within, 0, rows), 0)
    lane = jnp.arange(rows, dtype=jnp.int32)
    sorted_pos = jnp.clip((starts[tile_expert] + tile_within)[:, None] + lane[None, :],
                          0, n_pairs - 1)
    valid = lane[None, :] < tile_count[:, None]
    pair = order[sorted_pos]
    token = pair // TOP_K
    slot = pair % TOP_K
    src = jnp.where(valid, token, 0)
    dst = jnp.where(valid, slot * n + token, 0)
    gate = jnp.where(valid, w_sorted[sorted_pos], 0.0)
    return (src.reshape(n_tiles, 1, rows), dst.reshape(n_tiles, 1, rows),
            gate.reshape(n_tiles * rows, 1), tile_expert, tile_count)


def _expert_kernel(te_ref, tc_ref, src_ref, src_next_ref, dst_ref, gate_ref, h_hbm,
                   wg_ref, wu_ref, wd_ref, y_hbm,
                   gbuf_ref, ybuf_ref, hbf_ref, acc_ref, gsem, ssem, *,
                   rows, n_f, n_tiles, overlap):
    i = pl.program_id(0)
    j = pl.program_id(1)
    count = tc_ref[i]
    valid = count > 0
    full = count == rows
    first_step = j == 0
    last_step = j == n_f - 1
    has_next = (i + 1 < n_tiles) & (tc_ref[jnp.minimum(i + 1, n_tiles - 1)] > 0)
    prev_in_flight = (i > 0) & (tc_ref[jnp.maximum(i - 1, 0)] == rows)
    tile_rows = rows * V7X_SUBLANES

    def row_window(r):
        return pl.ds(pl.multiple_of(r * V7X_SUBLANES, V7X_SUBLANES), V7X_SUBLANES)

    def start_gather(idx_ref, r, slot):
        pltpu.make_async_copy(h_hbm.at[row_window(idx_ref[0, r]), :],
                              gbuf_ref.at[slot, row_window(r), :], gsem.at[slot]).start(
                                  priority=ROW_COPY_PRIORITY)

    def wait_gather(slot):
        pltpu.make_async_copy(h_hbm.at[pl.ds(0, tile_rows), :], gbuf_ref.at[slot],
                              gsem.at[slot]).wait()

    def scatter_copy(r):
        return pltpu.make_async_copy(ybuf_ref.at[row_window(r), :],
                                     y_hbm.at[row_window(dst_ref[0, r]), :], ssem.at[0])

    def wait_full_scatter():
        pltpu.make_async_copy(ybuf_ref, y_hbm.at[pl.ds(0, tile_rows), :], ssem.at[0]).wait()

    def for_all_rows(fn):
        def body(b, carry):
            for u in range(DMA_UNROLL):
                fn(b * DMA_UNROLL + u)
            return carry
        lax.fori_loop(0, rows // DMA_UNROLL, body, 0)

    def for_rows(n_rows, fn):
        def body(r, carry):
            fn(r)
            return carry
        lax.fori_loop(0, n_rows, body, 0)

    @pl.when(first_step & (i == 0))
    def _():
        for_all_rows(lambda r: start_gather(src_ref, r, 0))

    for slot in range(2):
        mine = valid & first_step & (i % 2 == slot)

        @pl.when(mine)
        def _(slot=slot):
            wait_gather(slot)
            hbf_ref[...] = _load_token_major(gbuf_ref.at[slot], rows).astype(BF16)
            acc_ref[...] = jnp.zeros_like(acc_ref)

        @pl.when(mine & has_next)
        def _(slot=slot):
            for_all_rows(lambda r: start_gather(src_next_ref, r, 1 - slot))

    def matmul_step(skip):
        h = hbf_ref[...]
        part = None
        width = wg_ref.shape[3]
        step = V7X_MXU_WIDTH * EXPERT_FF_TILES
        for c0 in range(skip, width, step):
            c1 = min(c0 + step, width)
            gate = _dot(h, wg_ref[0, 0, :, c0:c1].astype(BF16))
            up = _dot(h, wu_ref[0, 0, :, c0:c1].astype(BF16))
            act = (gate * jax.nn.sigmoid(gate) * up).astype(BF16)
            contrib = _dot(act, wd_ref[0, 0, c0:c1, :].astype(BF16))
            part = contrib if part is None else part + contrib
        acc_ref[...] += part

    upper = jnp.where(i % 2 == 0, j, n_f - 1 - j) == 1

    @pl.when(valid & jnp.logical_not(upper))
    def _():
        matmul_step(0)

    @pl.when(valid & upper)
    def _():
        matmul_step(overlap)

    @pl.when(valid & last_step & prev_in_flight)
    def _():
        wait_full_scatter()

    @pl.when(valid & last_step)
    def _():
        _store_token_major(ybuf_ref, acc_ref[...] * gate_ref[...], rows)

    @pl.when(valid & last_step & full)
    def _():
        for_all_rows(lambda r: scatter_copy(r).start(priority=ROW_COPY_PRIORITY))

    @pl.when(valid & last_step & full & jnp.logical_not(has_next))
    def _():
        wait_full_scatter()

    @pl.when(valid & last_step & jnp.logical_not(full))
    def _():
        for_rows(count, lambda r: scatter_copy(r).start())
        for_rows(count, lambda r: scatter_copy(r).wait())


def _experts(h_tm, idx, w, w_gu, w_down, layer, n):
    d = w_down.shape[3]
    d_ff = w_down.shape[2]
    rows = min(EXPERT_ROWS, n)
    assert rows % DMA_UNROLL == 0
    n_f = 2
    assert d_ff % V7X_MXU_WIDTH == 0 and d_ff // V7X_MXU_WIDTH >= n_f
    tf = -(-(d_ff // V7X_MXU_WIDTH) // n_f) * V7X_MXU_WIDTH
    overlap = n_f * tf - d_ff
    src, dst, gate, tile_expert, tile_count = _plan_routing(idx, w, rows)
    n_tiles = src.shape[0]

    def f_off(i, j, tc):
        pos = jnp.where(tc[i] > 0, jnp.where(i % 2 == 0, j, n_f - 1 - j), 0)
        return pl.multiple_of(pos * (d_ff - tf), V7X_MXU_WIDTH)

    def window(rows_cols, offsets):
        return pl.BlockSpec((pl.Element(1), pl.Element(1)) + tuple(pl.Element(s) for s in rows_cols),
                            offsets)

    idx_spec = functools.partial(pl.BlockSpec, (None, 1, rows), memory_space=pltpu.SMEM)
    grid_spec = pltpu.PrefetchScalarGridSpec(
        num_scalar_prefetch=2,
        grid=(n_tiles, n_f),
        in_specs=[
            idx_spec(lambda i, j, te, tc: (i, 0, 0)),
            idx_spec(lambda i, j, te, tc: (jnp.minimum(i + 1, n_tiles - 1), 0, 0)),
            idx_spec(lambda i, j, te, tc: (i, 0, 0)),
            pl.BlockSpec((rows, 1), lambda i, j, te, tc: (i, 0)),
            pl.BlockSpec(memory_space=pl.ANY),
            window((d, tf), lambda i, j, te, tc: (layer, te[i], 0, f_off(i, j, tc))),
            window((d, tf), lambda i, j, te, tc: (
                layer, te[i], 0, pl.multiple_of(d_ff + f_off(i, j, tc), V7X_MXU_WIDTH))),
            window((tf, d), lambda i, j, te, tc: (layer, te[i], f_off(i, j, tc), 0)),
        ],
        out_specs=pl.BlockSpec(memory_space=pl.ANY),
        scratch_shapes=[
            pltpu.VMEM((2, rows * V7X_SUBLANES, V7X_LANES), F32),
            pltpu.VMEM((rows * V7X_SUBLANES, V7X_LANES), F32),
            pltpu.VMEM((rows, d), BF16),
            pltpu.VMEM((rows, d), F32),
            pltpu.SemaphoreType.DMA((2,)),
            pltpu.SemaphoreType.DMA((1,)),
        ],
    )
    return pl.pallas_call(
        functools.partial(_expert_kernel, rows=rows, n_f=n_f, n_tiles=n_tiles, overlap=overlap),
        grid_spec=grid_spec,
        out_shape=jax.ShapeDtypeStruct((n * TOP_K * V7X_SUBLANES, V7X_LANES), F32),
        compiler_params=_params("arbitrary", "arbitrary"),
        name="routed_swiglu",
    )(tile_expert, tile_count, src, src, dst, gate, h_tm, w_gu, w_gu, w_down)


def _final_kernel(x_ref, y0_ref, y1_ref, g_ref, out_ref, *, rows):
    x = x_ref[...] + _load_token_major(y0_ref, rows) + _load_token_major(y1_ref, rows)
    out_ref[...] = _rms_unit(x) * g_ref[...]


def _final_norm(x, y, g):
    n, d = x.shape
    rows = min(PROJ_ROWS, n)
    blocks_per_slot = n // rows
    row_spec = pl.BlockSpec((rows, d), lambda i: (i, 0))
    return pl.pallas_call(
        functools.partial(_final_kernel, rows=rows),
        grid=(n // rows,),
        in_specs=[
            row_spec,
            pl.BlockSpec((rows * V7X_SUBLANES, V7X_LANES), lambda i: (i, 0)),
            pl.BlockSpec((rows * V7X_SUBLANES, V7X_LANES), lambda i: (i + blocks_per_slot, 0)),
            pl.BlockSpec((1, d), lambda i: (0, 0)),
        ],
        out_specs=row_spec,
        out_shape=jax.ShapeDtypeStruct((n, d), F32),
        compiler_params=_params("arbitrary"),
        name="final_norm",
    )(x, y, y, g.reshape(1, d))


def kernel(x, g_mix, g_ffn, g_final, a_w_in, a_conv_w, a_w_out, g_kv, w_kv, b_w_q, b_w_o,
           ffn_w_gu, ffn_w_down, moe_w_router, moe_w_gu, moe_w_down):
    batch, seq, d = x.shape
    n = batch * seq
    depth = g_mix.shape[0]
    n_self = a_w_in.shape[0]
    q_scale = LOG2_E / math.sqrt(d // N_HEADS)

    def bf(a):
        return a.astype(BF16)

    ffn_w_gu, ffn_w_down = bf(ffn_w_gu), bf(ffn_w_down)

    xs = x.reshape(n, d)
    y = None
    k = v = None
    for i in range(depth):
        o = w_o = None
        if i < n_self:
            assert y is None
            xs = _mixer(xs, g_mix[i], bf(a_w_in[i]), a_conv_w[i], bf(a_w_out[i]), batch, seq)
        else:
            j = i - n_self
            first = i == n_self
            outs = _proj(xs, g_mix[i], bf(b_w_q[j]), q_scale, y=y,
                         g_kv=g_kv if first else None, w_kv=bf(w_kv) if first else None)
            outs = list(outs)
            if y is not None:
                xs = outs.pop(0)
                y = None
            q = outs.pop(0)
            if first:
                k, v = outs
            o = _attention(q, k, v, batch, seq)
            w_o = bf(b_w_o[j])
        assert y is None
        if i % 2 == 0:
            xs = _ffn(xs, g_ffn[i], ffn_w_gu, ffn_w_down, i // 2, o=o, w_o=w_o)
        else:
            m = i // 2
            xs, h_tm, idx, w = _router(xs, g_ffn[i], moe_w_router[m], o=o, w_o=w_o)
            y = _experts(h_tm, idx, w, moe_w_gu, moe_w_down, m, n)
    assert y is not None
    return _final_norm(xs, y, g_final).reshape(batch, seq, d)
```

```python
import functools
import math

import jax
import jax.numpy as jnp
from jax import lax
from jax.experimental import pallas as pl
from jax.experimental.pallas import tpu as pltpu

F32 = jnp.float32
BF16 = jnp.bfloat16

RMS_EPS = 1e-6
LOG2_E = math.log2(math.e)
N_HEADS = 16
N_EXPERTS = 8
TOP_K = 2

V7X_LANES = 128
V7X_SUBLANES = 8
V7X_MXU_WIDTH = 256
V7X_VMEM_BYTES = 64 * 1024 * 1024
VMEM_LIMIT = V7X_VMEM_BYTES - 8 * 1024 * 1024

MIXER_ROWS = 1024
FFN_ROWS = 512
PROJ_ROWS = 1024
ROUTER_ROWS = 1024
EXPERT_ROWS = 512
ATTN_BLOCK = 256
ATTN_TILES = 4
ATTN_UNDERFLOW_LOG2 = -150.0
HEADS_PER_STEP = 2
CONV_CARRY_ROWS = V7X_SUBLANES
DMA_UNROLL = 64
EXPERT_FF_TILES = 3


def _params(*semantics):
    return pltpu.CompilerParams(dimension_semantics=semantics, vmem_limit_bytes=VMEM_LIMIT)


def _dot(a, b):
    return jnp.dot(a, b, preferred_element_type=F32)


def _rms_unit(x):
    return x * lax.rsqrt(jnp.mean(x * x, axis=-1, keepdims=True) + RMS_EPS)


def _load_token_major(ref, rows):
    parts = [ref[pl.ds(j, rows, stride=V7X_SUBLANES), :] for j in range(V7X_SUBLANES)]
    return jnp.concatenate(parts, axis=1)


def _store_token_major(ref, val, rows):
    for j in range(V7X_SUBLANES):
        ref[pl.ds(j, rows, stride=V7X_SUBLANES), :] = val[:, j * V7X_LANES:(j + 1) * V7X_LANES]


def _mixer_kernel(*refs, rows, d, d_ff):
    if d_ff is None:
        x_ref, g_ref, win_ref, cw_ref, wout_ref, out_ref, carry_ref = refs
    else:
        (x_ref, g_ref, win_ref, cw_ref, wout_ref, gf_ref, wgu_ref, wd_ref,
         out_ref, carry_ref) = refs
    s = pl.program_id(1)

    @pl.when(s == 0)
    def _():
        carry_ref[...] = jnp.zeros_like(carry_ref)

    x = x_ref[...]
    h = (_rms_unit(x) * g_ref[...]).astype(BF16)
    proj = _dot(h, win_ref[...])
    u = proj[:, :d] * proj[:, d:2 * d]
    b_gate = proj[:, 2 * d:]
    prev = carry_ref[...]
    row = lax.broadcasted_iota(jnp.int32, (rows, d), 0)
    u1 = jnp.where(row == 0, prev[CONV_CARRY_ROWS - 1:CONV_CARRY_ROWS, :], pltpu.roll(u, 1, 0))
    u2 = pltpu.roll(u, 2, 0)
    u2 = jnp.where(row == 0, prev[CONV_CARRY_ROWS - 2:CONV_CARRY_ROWS - 1, :], u2)
    u2 = jnp.where(row == 1, prev[CONV_CARRY_ROWS - 1:CONV_CARRY_ROWS, :], u2)
    carry_ref[...] = u[rows - CONV_CARRY_ROWS:, :]
    cw = cw_ref[...]
    conv = cw[0:1, :] * u2 + cw[1:2, :] * u1 + cw[2:3, :] * u
    mixed = (b_gate * conv).astype(BF16)
    x1 = x + _dot(mixed, wout_ref[...])
    if d_ff is None:
        out_ref[...] = x1
        return
    h2 = (_rms_unit(x1) * gf_ref[...]).astype(BF16)
    acc = x1
    for c0 in range(0, d_ff, V7X_MXU_WIDTH):
        c1 = c0 + V7X_MXU_WIDTH
        gate = _dot(h2, wgu_ref[:, c0:c1])
        up = _dot(h2, wgu_ref[:, d_ff + c0:d_ff + c1])
        act = (gate * jax.nn.sigmoid(gate) * up).astype(BF16)
        acc = acc + _dot(act, wd_ref[c0:c1, :])
    out_ref[...] = acc


def _mixer(x, g, w_in, conv_w, w_out, batch, seq, ffn=None):
    n, d = x.shape
    rows = min(MIXER_ROWS if ffn is None else FFN_ROWS, seq)
    tiles = seq // rows
    once = pl.Buffered(1)
    const = lambda shape: pl.BlockSpec(shape, lambda i, j: (0,) * len(shape), pipeline_mode=once)
    in_specs = [
        pl.BlockSpec((rows, d), lambda i, j: (i * tiles + j, 0)),
        pl.BlockSpec((1, d), lambda i, j: (0, 0)),
        const((d, 3 * d)),
        pl.BlockSpec((3, d), lambda i, j: (0, 0)),
        const((d, d)),
    ]
    args = [x, g.reshape(1, d), w_in, conv_w, w_out]
    d_ff = None
    if ffn is not None:
        g_f, w_gu, w_down, layer = ffn
        d_ff = w_down.shape[1]
        assert d_ff % V7X_MXU_WIDTH == 0
        in_specs += [
            pl.BlockSpec((1, d), lambda i, j: (0, 0)),
            pl.BlockSpec((None, d, 2 * d_ff), lambda i, j: (layer, 0, 0), pipeline_mode=once),
            pl.BlockSpec((None, d_ff, d), lambda i, j: (layer, 0, 0), pipeline_mode=once),
        ]
        args += [g_f.reshape(1, d), w_gu, w_down]
    return pl.pallas_call(
        functools.partial(_mixer_kernel, rows=rows, d=d, d_ff=d_ff),
        grid=(batch, tiles),
        in_specs=in_specs,
        out_specs=pl.BlockSpec((rows, d), lambda i, j: (i * tiles + j, 0)),
        out_shape=jax.ShapeDtypeStruct((n, d), F32),
        scratch_shapes=[pltpu.VMEM((CONV_CARRY_ROWS, d), F32)],
        compiler_params=_params("arbitrary", "arbitrary"),
        name="conv_mixer" if ffn is None else "conv_mixer_swiglu",
    )(*args)


def _ffn_kernel(*refs, has_o, d_ff):
    if has_o:
        x_ref, o_ref, wo_ref, g_ref, wgu_ref, wd_ref, out_ref = refs
    else:
        x_ref, g_ref, wgu_ref, wd_ref, out_ref = refs
    x = x_ref[...]
    if has_o:
        x = x + _dot(o_ref[...], wo_ref[...])
    h = (_rms_unit(x) * g_ref[...]).astype(BF16)
    acc = x
    for c0 in range(0, d_ff, V7X_MXU_WIDTH):
        c1 = c0 + V7X_MXU_WIDTH
        gate = _dot(h, wgu_ref[:, c0:c1])
        up = _dot(h, wgu_ref[:, d_ff + c0:d_ff + c1])
        act = (gate * jax.nn.sigmoid(gate) * up).astype(BF16)
        acc = acc + _dot(act, wd_ref[c0:c1, :])
    out_ref[...] = acc


def _ffn(x, g, w_gu, w_down, layer, o=None, w_o=None):
    n, d = x.shape
    d_ff = w_down.shape[1]
    assert d_ff % V7X_MXU_WIDTH == 0
    rows = min(FFN_ROWS, n)
    has_o = o is not None
    row_spec = pl.BlockSpec((rows, d), lambda i: (i, 0))
    once = pl.Buffered(1)
    in_specs = [row_spec]
    args = [x]
    if has_o:
        in_specs += [row_spec, pl.BlockSpec((d, d), lambda i: (0, 0), pipeline_mode=once)]
        args += [o, w_o]
    in_specs += [
        pl.BlockSpec((1, d), lambda i: (0, 0)),
        pl.BlockSpec((None, d, 2 * d_ff), lambda i: (layer, 0, 0), pipeline_mode=once),
        pl.BlockSpec((None, d_ff, d), lambda i: (layer, 0, 0), pipeline_mode=once),
    ]
    args += [g.reshape(1, d), w_gu, w_down]
    return pl.pallas_call(
        functools.partial(_ffn_kernel, has_o=has_o, d_ff=d_ff),
        grid=(n // rows,),
        in_specs=in_specs,
        out_specs=row_spec,
        out_shape=jax.ShapeDtypeStruct((n, d), F32),
        compiler_params=_params("arbitrary"),
        name="dense_swiglu",
    )(*args)


def _proj_kernel(*refs, has_y, has_kv, rows, d, q_scale):
    refs = list(refs)
    x_ref = refs.pop(0)
    if has_y:
        y0_ref = refs.pop(0)
        y1_ref = refs.pop(0)
    gq_ref = refs.pop(0)
    wq_ref = refs.pop(0)
    if has_kv:
        gkv_ref = refs.pop(0)
        wkv_ref = refs.pop(0)
    if has_y:
        xs_ref = refs.pop(0)
    q_ref = refs.pop(0)
    if has_kv:
        k_ref = refs.pop(0)
        v_ref = refs.pop(0)

    x = x_ref[...]
    if has_y:
        x = x + _load_token_major(y0_ref, rows) + _load_token_major(y1_ref, rows)
        xs_ref[...] = x
    xn = _rms_unit(x)
    q = _dot((xn * gq_ref[...]).astype(BF16), wq_ref[...])
    q_ref[...] = (q * q_scale).astype(BF16)
    if has_kv:
        kv = _dot((xn * gkv_ref[...]).astype(BF16), wkv_ref[...])
        k_ref[...] = kv[:, :d].astype(BF16)
        v_ref[...] = kv[:, d:].astype(BF16)


def _proj(x, g_q, w_q, q_scale, y=None, g_kv=None, w_kv=None):
    n, d = x.shape
    rows = min(PROJ_ROWS, n)
    has_y = y is not None
    has_kv = w_kv is not None
    row_spec = pl.BlockSpec((rows, d), lambda i: (i, 0))
    vec_spec = pl.BlockSpec((1, d), lambda i: (0, 0))
    in_specs = [row_spec]
    args = [x]
    if has_y:
        blocks_per_slot = n // rows
        tm_spec0 = pl.BlockSpec((rows * V7X_SUBLANES, V7X_LANES), lambda i: (i, 0))
        tm_spec1 = pl.BlockSpec((rows * V7X_SUBLANES, V7X_LANES),
                                lambda i: (i + blocks_per_slot, 0))
        in_specs += [tm_spec0, tm_spec1]
        args += [y, y]
    in_specs += [vec_spec, pl.BlockSpec((d, d), lambda i: (0, 0))]
    args += [g_q.reshape(1, d), w_q]
    if has_kv:
        in_specs += [vec_spec, pl.BlockSpec((d, 2 * d), lambda i: (0, 0))]
        args += [g_kv.reshape(1, d), w_kv]
    out_specs = []
    out_shape = []
    if has_y:
        out_specs.append(row_spec)
        out_shape.append(jax.ShapeDtypeStruct((n, d), F32))
    n_bf16 = 3 if has_kv else 1
    out_specs += [row_spec] * n_bf16
    out_shape += [jax.ShapeDtypeStruct((n, d), BF16)] * n_bf16
    return pl.pallas_call(
        functools.partial(_proj_kernel, has_y=has_y, has_kv=has_kv, rows=rows, d=d,
                          q_scale=q_scale),
        grid=(n // rows,),
        in_specs=in_specs,
        out_specs=out_specs,
        out_shape=out_shape,
        compiler_params=_params("arbitrary"),
        name="norm_qkv_proj",
    )(*args)


def _attn_kernel(q_ref, k_ref, v_ref, later_ref, o_ref, qs_ref, acc_ref, run_ref, *,
                 blk, dh, q_blocks):
    lane = lax.broadcasted_iota(jnp.int32, (1, HEADS_PER_STEP * dh), 1)
    head_mask = [(lane >= hh * dh) & (lane < (hh + 1) * dh) for hh in range(HEADS_PER_STEP)]
    stacked = HEADS_PER_STEP * blk

    def visit(work):
        later = later_ref[...]
        runs = {}
        totals = {}
        for slot, kb, masked in work:
            run = runs[slot] if slot in runs else run_ref[slot]
            start = pl.multiple_of(kb * blk, blk)
            k_blk = k_ref[pl.ds(start, blk), :]
            v_blk = v_ref[pl.ds(start, blk), :]
            z2 = lax.dot_general(qs_ref[slot], k_blk, (((1,), (1,)), ((), ())),
                                 preferred_element_type=F32)
            soft = jnp.log2(1.0 + jnp.exp2(-jnp.abs(z2)))
            log_beta = jnp.minimum(z2, 0.0) - soft
            log_1m_beta = log_beta - z2
            if masked:
                t_idx = lax.broadcasted_iota(jnp.int32, (stacked, blk), 0) & (blk - 1)
                s_idx = lax.broadcasted_iota(jnp.int32, (stacked, blk), 1)
                causal = s_idx < t_idx
                log_1m_beta = jnp.where(causal, log_1m_beta, 0.0)
            suffix = _dot(log_1m_beta.astype(BF16), later)
            attn = jnp.exp2(log_beta + suffix + run)
            if masked:
                attn = jnp.where(causal, attn, 0.0)
            av = _dot(attn.astype(BF16), v_blk)
            out = jnp.where(head_mask[0], av[:blk], av[blk:])
            totals[slot] = out if slot not in totals else totals[slot] + out
            runs[slot] = run + jnp.sum(log_1m_beta, axis=-1, keepdims=True)
        for slot in totals:
            acc_ref[slot] += totals[slot]
            run_ref[slot] = runs[slot]

    def exhausted(slot):
        return jnp.max(run_ref[slot]) <= ATTN_UNDERFLOW_LOG2

    def load_tile(slot, qi):
        q = q_ref[pl.ds(pl.multiple_of(qi * blk, blk), blk), :]
        for hh in range(HEADS_PER_STEP):
            qs_ref[slot, hh * blk:(hh + 1) * blk, :] = jnp.where(head_mask[hh], q, jnp.zeros_like(q))
        acc_ref[slot] = jnp.zeros_like(acc_ref[slot])
        run_ref[slot] = jnp.zeros_like(run_ref[slot])

    def finish_tile(slot, qi, done):
        def pair(it, done):
            kb = qi - 2 - 2 * it

            @pl.when(jnp.logical_not(done))
            def _():
                visit([(slot, kb, False), (slot, kb - 1, False)])

            return done | exhausted(slot)

        done = lax.fori_loop(0, jnp.maximum(qi - 1, 0) // 2, pair, done)

        @pl.when(jnp.logical_not(done) & (qi >= 2) & (qi % 2 == 0))
        def _():
            visit([(slot, 0, False)])

        o_ref[pl.ds(pl.multiple_of(qi * blk, blk), blk), :] = acc_ref[slot].astype(BF16)

    def tile_group(m, carry):
        tiles = [ATTN_TILES * m + t for t in range(ATTN_TILES)]
        for t, qi in enumerate(tiles):
            load_tile(t, qi)
        diagonals = [(t, qi, True) for t, qi in enumerate(tiles)]
        befores = [(t, qi - 1, False) for t, qi in enumerate(tiles)]

        @pl.when(m == 0)
        def _():
            visit(diagonals + befores[1:])

        @pl.when(m > 0)
        def _():
            visit(diagonals + befores)

        done = [exhausted(t) for t in range(ATTN_TILES)]
        for t, qi in enumerate(tiles):
            finish_tile(t, qi, done[t])
        return carry

    lax.fori_loop(0, q_blocks // ATTN_TILES, tile_group, 0)


def _attention(q, k, v, batch, seq):
    n, d = q.shape
    dh = d // N_HEADS
    blk = min(ATTN_BLOCK, seq)
    width = HEADS_PER_STEP * dh
    q_blocks = seq // blk
    assert q_blocks % ATTN_TILES == 0
    idx = jnp.arange(blk, dtype=jnp.int32)
    later = (idx[:, None] > idx[None, :]).astype(BF16)
    col_blocks = d // width
    seq_spec = pl.BlockSpec((seq, width), lambda g: (g // col_blocks, g % col_blocks))
    return pl.pallas_call(
        functools.partial(_attn_kernel, blk=blk, dh=dh, q_blocks=q_blocks),
        grid=(batch * col_blocks,),
        in_specs=[seq_spec, seq_spec, seq_spec, pl.BlockSpec((blk, blk), lambda g: (0, 0))],
        out_specs=seq_spec,
        out_shape=jax.ShapeDtypeStruct((n, d), BF16),
        scratch_shapes=[pltpu.VMEM((ATTN_TILES, HEADS_PER_STEP * blk, width), BF16),
                        pltpu.VMEM((ATTN_TILES, blk, width), F32),
                        pltpu.VMEM((ATTN_TILES, HEADS_PER_STEP * blk, 1), F32)],
        compiler_params=_params("arbitrary"),
        name="stick_breaking_attention",
    )(q, k, v, later)


def _router_kernel(*refs, has_o, rows):
    refs = list(refs)
    x_ref = refs.pop(0)
    if has_o:
        o_ref = refs.pop(0)
        wo_ref = refs.pop(0)
    g_ref = refs.pop(0)
    wr_ref = refs.pop(0)
    if has_o:
        xs_ref = refs.pop(0)
    h_ref, idx_ref, w_ref = refs

    x = x_ref[...]
    if has_o:
        x = x + _dot(o_ref[...], wo_ref[...])
        xs_ref[...] = x
    h = _rms_unit(x) * g_ref[...]
    _store_token_major(h_ref, h, rows)

    h_hi = h.astype(BF16)
    h_lo = (h - h_hi.astype(F32)).astype(BF16)
    both = _dot(h_hi, wr_ref[...])
    logits = both[:, :V7X_LANES] + (both[:, V7X_LANES:] + _dot(h_lo, wr_ref[:, :V7X_LANES]))

    lane = lax.broadcasted_iota(jnp.int32, logits.shape, 1)
    lane_f = lane.astype(F32)
    neg_inf = jnp.float32(-jnp.inf)
    no_lane = jnp.float32(V7X_LANES)
    lg = jnp.where(lane < N_EXPERTS, logits, neg_inf)
    m1 = jnp.max(lg, axis=-1, keepdims=True)
    i1 = jnp.min(jnp.where(lg == m1, lane_f, no_lane), axis=-1, keepdims=True)
    lg2 = jnp.where(lane_f == i1, neg_inf, lg)
    m2 = jnp.max(lg2, axis=-1, keepdims=True)
    i2 = jnp.min(jnp.where(lg2 == m2, lane_f, no_lane), axis=-1, keepdims=True)
    e2 = jnp.exp(m2 - m1)
    w1 = 1.0 / (1.0 + e2)
    w2 = e2 * w1
    idx_ref[...] = jnp.where(lane == 0, i1, jnp.where(lane == 1, i2, 0.0)).astype(jnp.int32)
    w_ref[...] = jnp.where(lane == 0, w1, jnp.where(lane == 1, w2, 0.0))


def _router(x, g, w_router, o=None, w_o=None):
    n, d = x.shape
    rows = min(ROUTER_ROWS, n)
    has_o = o is not None
    wr = jnp.zeros((d, V7X_LANES), F32).at[:, :N_EXPERTS].set(w_router)
    wr_hi = wr.astype(BF16)
    wr_lo = (wr - wr_hi.astype(F32)).astype(BF16)
    wr_both = jnp.concatenate([wr_hi, wr_lo], axis=1)
    row_spec = pl.BlockSpec((rows, d), lambda i: (i, 0))
    lane_spec = pl.BlockSpec((rows, V7X_LANES), lambda i: (i, 0))
    in_specs = [row_spec]
    args = [x]
    if has_o:
        in_specs += [row_spec, pl.BlockSpec((d, d), lambda i: (0, 0))]
        args += [o, w_o]
    in_specs += [pl.BlockSpec((1, d), lambda i: (0, 0)),
                 pl.BlockSpec((d, 2 * V7X_LANES), lambda i: (0, 0))]
    args += [g.reshape(1, d), wr_both]
    out_specs = []
    out_shape = []
    if has_o:
        out_specs.append(row_spec)
        out_shape.append(jax.ShapeDtypeStruct((n, d), F32))
    out_specs += [pl.BlockSpec((rows * V7X_SUBLANES, V7X_LANES), lambda i: (i, 0)),
                  lane_spec, lane_spec]
    out_shape += [jax.ShapeDtypeStruct((n * V7X_SUBLANES, V7X_LANES), F32),
                  jax.ShapeDtypeStruct((n, V7X_LANES), jnp.int32),
                  jax.ShapeDtypeStruct((n, V7X_LANES), F32)]
    outs = pl.pallas_call(
        functools.partial(_router_kernel, has_o=has_o, rows=rows),
        grid=(n // rows,),
        in_specs=in_specs,
        out_specs=out_specs,
        out_shape=out_shape,
        compiler_params=_params("arbitrary"),
        name="router_top2",
    )(*args)
    if has_o:
        xs, h_tm, idx, w = outs
    else:
        h_tm, idx, w = outs
        xs = x
    return xs, h_tm, idx[:, :TOP_K], w[:, :TOP_K]


def _plan_routing(idx, w, rows):
    n = idx.shape[0]
    n_pairs = n * TOP_K
    n_tiles = n_pairs // rows + N_EXPERTS
    e_flat = idx.reshape(n_pairs)
    pair_id = jnp.arange(n_pairs, dtype=jnp.int32)
    _, order, w_sorted = lax.sort((e_flat, pair_id, w.reshape(n_pairs)), num_keys=1, is_stable=True)
    experts = jnp.arange(N_EXPERTS, dtype=jnp.int32)
    counts = jnp.sum((e_flat[:, None] == experts[None, :]).astype(jnp.int32), axis=0)
    starts = jnp.cumsum(counts) - counts
    pad_counts = ((counts + rows - 1) // rows) * rows
    pad_ends = jnp.cumsum(pad_counts)
    pad_starts = pad_ends - pad_counts
    tile_start = jnp.arange(n_tiles, dtype=jnp.int32) * rows
    e_of_tile = jnp.sum((tile_start[:, None] >= pad_ends[None, :]).astype(jnp.int32), axis=1)
    tile_expert = jnp.minimum(e_of_tile, N_EXPERTS - 1)
    tile_within = tile_start - pad_starts[tile_expert]
    tile_count = jnp.where(e_of_tile < N_EXPERTS,
                           jnp.clip(counts[tile_expert] - tile_within, 0, rows), 0)
    lane = jnp.arange(rows, dtype=jnp.int32)
    sorted_pos = jnp.clip((starts[tile_expert] + tile_within)[:, None] + lane[None, :],
                          0, n_pairs - 1)
    valid = lane[None, :] < tile_count[:, None]
    pair = order[sorted_pos]
    token = pair // TOP_K
    slot = pair % TOP_K
    src = jnp.where(valid, token, 0)
    dst = jnp.where(valid, slot * n + token, 0)
    gate = jnp.where(valid, w_sorted[sorted_pos], 0.0)
    return (src.reshape(n_tiles, 1, rows), dst.reshape(n_tiles, 1, rows),
            gate.reshape(n_tiles * rows, 1), tile_expert, tile_count)


def _expert_kernel(te_ref, tc_ref, src_ref, src_next_ref, dst_ref, gate_ref, h_hbm,
                   wg_ref, wu_ref, wd_ref, y_hbm,
                   gbuf_ref, ybuf_ref, hbf_ref, acc_ref, gsem, ssem, *,
                   rows, n_f, n_tiles, overlap):
    i = pl.program_id(0)
    j = pl.program_id(1)
    count = tc_ref[i]
    valid = count > 0
    full = count == rows
    first_step = j == 0
    last_step = j == n_f - 1
    has_next = (i + 1 < n_tiles) & (tc_ref[jnp.minimum(i + 1, n_tiles - 1)] > 0)
    prev_in_flight = (i > 0) & (tc_ref[jnp.maximum(i - 1, 0)] == rows)
    tile_rows = rows * V7X_SUBLANES

    def row_window(r):
        return pl.ds(pl.multiple_of(r * V7X_SUBLANES, V7X_SUBLANES), V7X_SUBLANES)

    def start_gather(idx_ref, r, slot):
        pltpu.make_async_copy(h_hbm.at[row_window(idx_ref[0, r]), :],
                              gbuf_ref.at[slot, row_window(r), :], gsem.at[slot]).start()

    def wait_gather(slot):
        pltpu.make_async_copy(h_hbm.at[pl.ds(0, tile_rows), :], gbuf_ref.at[slot],
                              gsem.at[slot]).wait()

    def scatter_copy(r):
        return pltpu.make_async_copy(ybuf_ref.at[row_window(r), :],
                                     y_hbm.at[row_window(dst_ref[0, r]), :], ssem.at[0])

    def wait_full_scatter():
        pltpu.make_async_copy(ybuf_ref, y_hbm.at[pl.ds(0, tile_rows), :], ssem.at[0]).wait()

    def for_all_rows(fn):
        def body(b, carry):
            for u in range(DMA_UNROLL):
                fn(b * DMA_UNROLL + u)
            return carry
        lax.fori_loop(0, rows // DMA_UNROLL, body, 0)

    def for_rows(n_rows, fn):
        def body(r, carry):
            fn(r)
            return carry
        lax.fori_loop(0, n_rows, body, 0)

    @pl.when(first_step & (i == 0))
    def _():
        for_all_rows(lambda r: start_gather(src_ref, r, 0))

    for slot in range(2):
        mine = valid & first_step & (i % 2 == slot)

        @pl.when(mine)
        def _(slot=slot):
            wait_gather(slot)
            hbf_ref[...] = _load_token_major(gbuf_ref.at[slot], rows).astype(BF16)
            acc_ref[...] = jnp.zeros_like(acc_ref)

        @pl.when(mine & has_next)
        def _(slot=slot):
            for_all_rows(lambda r: start_gather(src_next_ref, r, 1 - slot))

    def matmul_step(skip):
        h = hbf_ref[...]
        part = None
        width = wg_ref.shape[3]
        step = V7X_MXU_WIDTH * EXPERT_FF_TILES
        for c0 in range(skip, width, step):
            c1 = min(c0 + step, width)
            gate = _dot(h, wg_ref[0, 0, :, c0:c1].astype(BF16))
            up = _dot(h, wu_ref[0, 0, :, c0:c1].astype(BF16))
            act = (gate * jax.nn.sigmoid(gate) * up).astype(BF16)
            contrib = _dot(act, wd_ref[0, 0, c0:c1, :].astype(BF16))
            part = contrib if part is None else part + contrib
        acc_ref[...] += part

    upper = jnp.where(i % 2 == 0, j, n_f - 1 - j) == 1

    @pl.when(valid & jnp.logical_not(upper))
    def _():
        matmul_step(0)

    @pl.when(valid & upper)
    def _():
        matmul_step(overlap)

    @pl.when(valid & last_step & prev_in_flight)
    def _():
        wait_full_scatter()

    @pl.when(valid & last_step)
    def _():
        _store_token_major(ybuf_ref, acc_ref[...] * gate_ref[...], rows)

    @pl.when(valid & last_step & full)
    def _():
        for_all_rows(lambda r: scatter_copy(r).start())

    @pl.when(valid & last_step & full & jnp.logical_not(has_next))
    def _():
        wait_full_scatter()

    @pl.when(valid & last_step & jnp.logical_not(full))
    def _():
        for_rows(count, lambda r: scatter_copy(r).start())
        for_rows(count, lambda r: scatter_copy(r).wait())


def _experts(h_tm, idx, w, w_gu, w_down, layer, n):
    d = w_down.shape[3]
    d_ff = w_down.shape[2]
    rows = min(EXPERT_ROWS, n)
    assert rows % DMA_UNROLL == 0
    n_f = 2
    assert d_ff % V7X_MXU_WIDTH == 0 and d_ff // V7X_MXU_WIDTH >= n_f
    tf = -(-(d_ff // V7X_MXU_WIDTH) // n_f) * V7X_MXU_WIDTH
    overlap = n_f * tf - d_ff
    src, dst, gate, tile_expert, tile_count = _plan_routing(idx, w, rows)
    n_tiles = src.shape[0]

    def f_off(i, j, tc):
        pos = jnp.where(tc[i] > 0, jnp.where(i % 2 == 0, j, n_f - 1 - j), 0)
        return pl.multiple_of(pos * (d_ff - tf), V7X_MXU_WIDTH)

    def window(rows_cols, offsets):
        return pl.BlockSpec((pl.Element(1), pl.Element(1)) + tuple(pl.Element(s) for s in rows_cols),
                            offsets)

    idx_spec = functools.partial(pl.BlockSpec, (None, 1, rows), memory_space=pltpu.SMEM)
    grid_spec = pltpu.PrefetchScalarGridSpec(
        num_scalar_prefetch=2,
        grid=(n_tiles, n_f),
        in_specs=[
            idx_spec(lambda i, j, te, tc: (i, 0, 0)),
            idx_spec(lambda i, j, te, tc: (jnp.minimum(i + 1, n_tiles - 1), 0, 0)),
            idx_spec(lambda i, j, te, tc: (i, 0, 0)),
            pl.BlockSpec((rows, 1), lambda i, j, te, tc: (i, 0)),
            pl.BlockSpec(memory_space=pl.ANY),
            window((d, tf), lambda i, j, te, tc: (layer, te[i], 0, f_off(i, j, tc))),
            window((d, tf), lambda i, j, te, tc: (
                layer, te[i], 0, pl.multiple_of(d_ff + f_off(i, j, tc), V7X_MXU_WIDTH))),
            window((tf, d), lambda i, j, te, tc: (layer, te[i], f_off(i, j, tc), 0)),
        ],
        out_specs=pl.BlockSpec(memory_space=pl.ANY),
        scratch_shapes=[
            pltpu.VMEM((2, rows * V7X_SUBLANES, V7X_LANES), F32),
            pltpu.VMEM((rows * V7X_SUBLANES, V7X_LANES), F32),
            pltpu.VMEM((rows, d), BF16),
            pltpu.VMEM((rows, d), F32),
            pltpu.SemaphoreType.DMA((2,)),
            pltpu.SemaphoreType.DMA((1,)),
        ],
    )
    return pl.pallas_call(
        functools.partial(_expert_kernel, rows=rows, n_f=n_f, n_tiles=n_tiles, overlap=overlap),
        grid_spec=grid_spec,
        out_shape=jax.ShapeDtypeStruct((n * TOP_K * V7X_SUBLANES, V7X_LANES), F32),
        compiler_params=_params("arbitrary", "arbitrary"),
        name="routed_swiglu",
    )(tile_expert, tile_count, src, src, dst, gate, h_tm, w_gu, w_gu, w_down)


def _final_kernel(x_ref, y0_ref, y1_ref, g_ref, out_ref, *, rows):
    x = x_ref[...] + _load_token_major(y0_ref, rows) + _load_token_major(y1_ref, rows)
    out_ref[...] = _rms_unit(x) * g_ref[...]


def _final_norm(x, y, g):
    n, d = x.shape
    rows = min(PROJ_ROWS, n)
    blocks_per_slot = n // rows
    row_spec = pl.BlockSpec((rows, d), lambda i: (i, 0))
    return pl.pallas_call(
        functools.partial(_final_kernel, rows=rows),
        grid=(n // rows,),
        in_specs=[
            row_spec,
            pl.BlockSpec((rows * V7X_SUBLANES, V7X_LANES), lambda i: (i, 0)),
            pl.BlockSpec((rows * V7X_SUBLANES, V7X_LANES), lambda i: (i + blocks_per_slot, 0)),
            pl.BlockSpec((1, d), lambda i: (0, 0)),
        ],
        out_specs=row_spec,
        out_shape=jax.ShapeDtypeStruct((n, d), F32),
        compiler_params=_params("arbitrary"),
        name="final_norm",
    )(x, y, y, g.reshape(1, d))


def kernel(x, g_mix, g_ffn, g_final, a_w_in, a_conv_w, a_w_out, g_kv, w_kv, b_w_q, b_w_o,
           ffn_w_gu, ffn_w_down, moe_w_router, moe_w_gu, moe_w_down):
    batch, seq, d = x.shape
    n = batch * seq
    depth = g_mix.shape[0]
    n_self = a_w_in.shape[0]
    q_scale = LOG2_E / math.sqrt(d // N_HEADS)

    def bf(a):
        return a.astype(BF16)

    ffn_w_gu, ffn_w_down = bf(ffn_w_gu), bf(ffn_w_down)

    xs = x.reshape(n, d)
    y = None
    k = v = None
    for i in range(depth):
        o = w_o = None
        if i < n_self:
            assert y is None
            ffn = (g_ffn[i], ffn_w_gu, ffn_w_down, i // 2) if i % 2 == 0 else None
            xs = _mixer(xs, g_mix[i], bf(a_w_in[i]), a_conv_w[i], bf(a_w_out[i]), batch, seq,
                        ffn=ffn)
            if ffn is not None:
                continue
        else:
            j = i - n_self
            first = i == n_self
            outs = _proj(xs, g_mix[i], bf(b_w_q[j]), q_scale, y=y,
                         g_kv=g_kv if first else None, w_kv=bf(w_kv) if first else None)
            outs = list(outs)
            if y is not None:
                xs = outs.pop(0)
                y = None
            q = outs.pop(0)
            if first:
                k, v = outs
            o = _attention(q, k, v, batch, seq)
            w_o = bf(b_w_o[j])
        assert y is None
        if i % 2 == 0:
            xs = _ffn(xs, g_ffn[i], ffn_w_gu, ffn_w_down, i // 2, o=o, w_o=w_o)
        else:
            m = i // 2
            xs, h_tm, idx, w = _router(xs, g_ffn[i], moe_w_router[m], o=o, w_o=w_o)
            y = _experts(h_tm, idx, w, moe_w_gu, moe_w_down, m, n)
    assert y is not None
    return _final_norm(xs, y, g_final).reshape(batch, seq, d)
```
